```python
import math
import jax
import jax.numpy as jnp
from jax import lax
import numpy as np

D_MODEL = 2048
BATCH = 4
SEQ = 2048
DEPTH = 2

HEAD_DIM = 64
SCALE = HEAD_DIM ** -0.5
Q_BLOCK = 128
NORM_EPS = 1e-6
NEG = -1e30
TINY = 1e-30

DIL_GROUPS = ((128, 1), (512, 4), (2048, 16))
N_DIL_GROUPS = 3
A_HEADS = 8
A_WIDTH = A_HEADS * HEAD_DIM

B_HEADS = 16
B_KV_GROUPS = 2
B_HEADS_PER_GROUP = B_HEADS // B_KV_GROUPS
CMP_LEN = 32
CMP_STRIDE = 16
CMP_HIDDEN = 256
SEL_LEN = 64
SEL_TOPK = 16
SEL_LOCAL = 2
FORCED_SCORE = 1e6
WIN = 512
B_WIDTH = B_HEADS * HEAD_DIM

C_HEADS = 8
C_VDIM = 2 * HEAD_DIM
C_WIDTH = C_HEADS * C_VDIM

N_BUCKETS = 32
BIAS_MAX_DIST = 2048
N_BIAS_A = N_DIL_GROUPS * A_HEADS
N_BIAS_HEADS = N_BIAS_A + B_HEADS + C_HEADS

D_FF = 4 * D_MODEL
N_BRANCHES = 3

IN_SPLITS = (
    3 * N_DIL_GROUPS * A_HEADS * HEAD_DIM,
    B_HEADS * HEAD_DIM,
    6 * B_KV_GROUPS * HEAD_DIM,
    3 * B_HEADS,
    C_HEADS * 2 * HEAD_DIM,
    C_HEADS * 2 * HEAD_DIM,
    C_HEADS * C_VDIM,
    N_BRANCHES * D_MODEL,
)
IN_COLS = sum(IN_SPLITS)
IN_OFFSETS = tuple(int(v) for v in np.cumsum(IN_SPLITS)[:-1])

kernel_name = "hybrid_gated_dilated_nsa_diff_block"


def rms_norm(x, gain):
    xf = x.astype(jnp.float32)
    y = xf * lax.rsqrt(jnp.mean(xf * xf, axis=-1, keepdims=True) + NORM_EPS)
    return (y * gain.astype(jnp.float32)).astype(x.dtype)


def masked_softmax(logits, mask):
    logits = jnp.where(mask, logits, NEG)
    m = jnp.max(logits, axis=-1, keepdims=True)
    e = jnp.where(mask, jnp.exp(logits - m), 0.0)
    s = jnp.maximum(jnp.sum(e, axis=-1, keepdims=True), TINY)
    return e / s, (m + jnp.log(s))[..., 0]


def t5_bucket(dist):
    exact = N_BUCKETS // 2
    d = jnp.maximum(dist, 0)
    logd = jnp.log(jnp.maximum(d, 1).astype(jnp.float32) / exact)
    far = exact + (logd / math.log(BIAS_MAX_DIST / exact) * (N_BUCKETS - exact)).astype(jnp.int32)
    return jnp.where(d < exact, d, jnp.minimum(far, N_BUCKETS - 1))


def sweep_query_blocks(block_fn, seq):
    out = lax.map(block_fn, jnp.arange(seq // Q_BLOCK))
    out = jnp.moveaxis(out, 0, 1)
    return out.reshape((out.shape[0], seq) + out.shape[3:])


def dilated_attention(q, k, v, rel_a):
    seq = q.shape[1]

    def block(bi):
        t = bi * Q_BLOCK + jnp.arange(Q_BLOCK)
        qb = lax.dynamic_slice_in_dim(q, bi * Q_BLOCK, Q_BLOCK, axis=1)
        outs, lses = [], []
        for g, (win, dil) in enumerate(DIL_GROUPS):
            offs = jnp.arange(win // dil + 1) * dil
            idx = t[:, None] - offs[None, :]
            valid = idx >= 0
            idx = jnp.maximum(idx, 0)
            kg = k[:, :, g][:, idx]
            vg = v[:, :, g][:, idx]
            logits = jnp.einsum("bqhd,bqjhd->bhqj", qb[:, :, g], kg).astype(jnp.float32) * SCALE
            bias = rel_a[t5_bucket(offs), g].T.astype(jnp.float32)
            p, lse = masked_softmax(logits + bias[None, :, None, :], valid[None, None])
            outs.append(jnp.einsum("bhqj,bqjhd->bqhd", p.astype(v.dtype), vg))
            lses.append(jnp.transpose(lse, (0, 2, 1)))
        wts = jax.nn.softmax(jnp.stack(lses, axis=0), axis=0).astype(v.dtype)
        return (wts[0][..., None] * outs[0] + wts[1][..., None] * outs[1]
                + wts[2][..., None] * outs[2])

    return sweep_query_blocks(block, seq)


def nsa_attention(q, kv, gates, cmp_k_pos, cmp_v_pos, cmp_k_w1, cmp_k_w2, cmp_v_w1, cmp_v_w2, rel_b):
    bsz, seq = q.shape[:2]
    G, Hg, hd = B_KV_GROUPS, B_HEADS_PER_GROUP, HEAD_DIM
    qg = q.reshape(bsz, seq, G, Hg, hd)
    k_cmp, v_cmp, k_sel, v_sel, k_win, v_win = [kv[:, :, i] for i in range(6)]
    t_all = jnp.arange(seq)

    n_cmp = (seq - CMP_LEN) // CMP_STRIDE + 1
    starts = jnp.arange(n_cmp) * CMP_STRIDE
    cidx = starts[:, None] + jnp.arange(CMP_LEN)[None, :]

    def compress(kx, pos, w1, w2):
        blocks = kx[:, cidx] + pos[None, None, :, None, :]
        blocks = jnp.transpose(blocks, (0, 1, 3, 2, 4)).reshape(bsz, n_cmp, G, CMP_LEN * hd)
        return jax.nn.gelu(blocks @ w1) @ w2

    kc = compress(k_cmp, cmp_k_pos, cmp_k_w1, cmp_k_w2)
    vc = compress(v_cmp, cmp_v_pos, cmp_v_w1, cmp_v_w2)
    cmask = (starts + CMP_LEN - 1)[None, :] <= t_all[:, None]
    logits_c = jnp.einsum("bsghd,bngd->bghsn", qg, kc).astype(jnp.float32) * SCALE
    p_cmp, _ = masked_softmax(logits_c, cmask)
    o_cmp = jnp.einsum("bghsn,bngd->bsghd", p_cmp.astype(v_cmp.dtype), vc)

    n_sel = seq // SEL_LEN
    top = min(SEL_TOPK, n_sel)
    bstart = jnp.arange(n_sel) * SEL_LEN
    overlap = ((starts[:, None] < bstart[None, :] + SEL_LEN)
               & (starts[:, None] + CMP_LEN > bstart[None, :])).astype(jnp.float32)
    score = jnp.einsum("bghsn,nj->bgsj", p_cmp, overlap)
    cur = t_all // SEL_LEN
    back = cur[:, None] - jnp.arange(n_sel)[None, :]
    forced = (jnp.arange(n_sel)[None, :] == 0) | ((back >= 0) & (back < SEL_LOCAL))
    score = jnp.where(forced, FORCED_SCORE, score)
    score = jnp.where(bstart[None, :] <= t_all[:, None], score, NEG)
    _, sel_idx = lax.top_k(score, top)

    k_blk = jnp.transpose(k_sel.reshape(bsz, n_sel, SEL_LEN, G, hd), (0, 3, 1, 2, 4))
    v_blk = jnp.transpose(v_sel.reshape(bsz, n_sel, SEL_LEN, G, hd), (0, 3, 1, 2, 4))
    bidx = jnp.arange(bsz)[:, None, None, None]
    gidx = jnp.arange(G)[None, :, None, None]
    rel_bg = jnp.transpose(rel_b, (1, 0, 2))

    def sel_block(bi):
        qs = bi * Q_BLOCK
        t = qs + jnp.arange(Q_BLOCK)
        qb = lax.dynamic_slice_in_dim(qg, qs, Q_BLOCK, axis=1)
        ib = lax.dynamic_slice_in_dim(sel_idx, qs, Q_BLOCK, axis=2)
        kb = k_blk[bidx, gidx, ib].reshape(bsz, G, Q_BLOCK, top * SEL_LEN, hd)
        vb = v_blk[bidx, gidx, ib].reshape(bsz, G, Q_BLOCK, top * SEL_LEN, hd)
        kpos = (ib[..., None] * SEL_LEN + jnp.arange(SEL_LEN)).reshape(bsz, G, Q_BLOCK, top * SEL_LEN)
        dist = t[None, None, :, None] - kpos
        bias = rel_bg[gidx, t5_bucket(dist)]
        logits = (jnp.einsum("bqghd,bgqkd->bghqk", qb, kb).astype(jnp.float32) * SCALE
                  + jnp.transpose(bias, (0, 1, 4, 2, 3)).astype(jnp.float32))
        p, _ = masked_softmax(logits, (dist >= 0)[:, :, None])
        return jnp.einsum("bghqk,bgqkd->bqghd", p.astype(vb.dtype), vb)

    o_sel = sweep_query_blocks(sel_block, seq)

    kw = jnp.pad(k_win, ((0, 0), (WIN, 0), (0, 0), (0, 0)))
    vw = jnp.pad(v_win, ((0, 0), (WIN, 0), (0, 0), (0, 0)))

    def win_block(bi):
        qs = bi * Q_BLOCK
        t = qs + jnp.arange(Q_BLOCK)
        qb = lax.dynamic_slice_in_dim(qg, qs, Q_BLOCK, axis=1)
        kb = lax.dynamic_slice_in_dim(kw, qs, Q_BLOCK + WIN, axis=1)
        vb = lax.dynamic_slice_in_dim(vw, qs, Q_BLOCK + WIN, axis=1)
        kpos = qs - WIN + jnp.arange(Q_BLOCK + WIN)
        dist = t[:, None] - kpos[None, :]
        valid = (dist >= 0) & (dist < WIN) & (kpos[None, :] >= 0)
        bias = jnp.transpose(rel_b[t5_bucket(dist)], (2, 3, 0, 1)).astype(jnp.float32)
        logits = jnp.einsum("bqghd,bkgd->bghqk", qb, kb).astype(jnp.float32) * SCALE + bias
        p, _ = masked_softmax(logits, valid)
        return jnp.einsum("bghqk,bkgd->bqghd", p.astype(vb.dtype), vb)

    o_win = sweep_query_blocks(win_block, seq)

    shp = (bsz, seq, B_HEADS, hd)
    out = (gates[..., 0:1] * o_cmp.reshape(shp) + gates[..., 1:2] * o_sel.reshape(shp)
           + gates[..., 2:3] * o_win.reshape(shp))
    return out.reshape(bsz, seq, B_WIDTH)


def diff_attention(q, k, v, lam_vecs, sub_gain, rel_c, lam_init):
    bsz, seq = q.shape[:2]
    lv = lam_vecs.astype(jnp.float32)
    lam = jnp.exp(jnp.sum(lv[0] * lv[1])) - jnp.exp(jnp.sum(lv[2] * lv[3])) + lam_init
    kpos = jnp.arange(seq)

    def block(bi):
        t = bi * Q_BLOCK + jnp.arange(Q_BLOCK)
        qb = lax.dynamic_slice_in_dim(q, bi * Q_BLOCK, Q_BLOCK, axis=1)
        dist = t[:, None] - kpos[None, :]
        bias = jnp.transpose(rel_c[t5_bucket(dist)], (2, 0, 1)).astype(jnp.float32)
        logits = jnp.einsum("bqhcd,bkhcd->bchqk", qb, k).astype(jnp.float32) * SCALE + bias
        p, _ = masked_softmax(logits, dist >= 0)
        attn = p[:, 0] - lam * p[:, 1]
        return jnp.einsum("bhqk,bkhd->bqhd", attn.astype(v.dtype), v)

    o = sweep_query_blocks(block, seq)
    o = rms_norm(o, sub_gain) * (1.0 - lam_init)
    return o.reshape(bsz, seq, C_WIDTH)


def setup_inputs(seed: int = 0) -> dict:
    key = jax.random.key(seed)
    ks = jax.random.split(key, 24)
    f32 = jnp.float32
    L = DEPTH

    def w(k, shape, fan_in):
        return jax.random.normal(k, shape, f32) * fan_in ** -0.5

    def gain(k, shape):
        return 1.0 + 0.02 * jax.random.normal(k, shape, f32)

    return {
        "x": jax.random.normal(ks[0], (BATCH, SEQ, D_MODEL), f32),
        "rel_bias": 0.2 * jax.random.normal(ks[1], (N_BUCKETS, N_BIAS_HEADS), f32),
        "w_in": w(ks[2], (L, D_MODEL, IN_COLS), D_MODEL),
        "cmp_k_pos": 0.2 * jax.random.normal(ks[3], (L, CMP_LEN, HEAD_DIM), f32),
        "cmp_v_pos": 0.2 * jax.random.normal(ks[4], (L, CMP_LEN, HEAD_DIM), f32),
        "cmp_k_w1": w(ks[5], (L, CMP_LEN * HEAD_DIM, CMP_HIDDEN), CMP_LEN * HEAD_DIM),
        "cmp_k_w2": w(ks[6], (L, CMP_HIDDEN, HEAD_DIM), CMP_HIDDEN),
        "cmp_v_w1": w(ks[7], (L, CMP_LEN * HEAD_DIM, CMP_HIDDEN), CMP_LEN * HEAD_DIM),
        "cmp_v_w2": w(ks[8], (L, CMP_HIDDEN, HEAD_DIM), CMP_HIDDEN),
        "diff_lambda": 0.1 * jax.random.normal(ks[9], (L, 4, HEAD_DIM), f32),
        "diff_norm": gain(ks[10], (L, C_VDIM)),
        "w_branch_a": w(ks[11], (L, A_WIDTH, D_MODEL), A_WIDTH),
        "w_branch_b": w(ks[12], (L, B_WIDTH, D_MODEL), B_WIDTH),
        "w_branch_c": w(ks[13], (L, C_WIDTH, D_MODEL), C_WIDTH),
        "w_out": w(ks[14], (L, D_MODEL, D_MODEL), D_MODEL),
        "norm_mix_pre": gain(ks[15], (L, D_MODEL)),
        "norm_mix_post": gain(ks[16], (L, D_MODEL)),
        "norm_mlp_pre": gain(ks[17], (L, D_MODEL)),
        "norm_mlp_post": gain(ks[18], (L, D_MODEL)),
        "w_up": w(ks[19], (L, D_MODEL, D_FF), D_MODEL),
        "w_down": w(ks[20], (L, D_FF, D_MODEL), D_FF),
    }


def reference(x, rel_bias, w_in, cmp_k_pos, cmp_v_pos, cmp_k_w1, cmp_k_w2, cmp_v_w1, cmp_v_w2,
              diff_lambda, diff_norm, w_branch_a, w_branch_b, w_branch_c, w_out,
              norm_mix_pre, norm_mix_post, norm_mlp_pre, norm_mlp_post, w_up, w_down):
    bsz, seq, _ = x.shape
    rel_a = rel_bias[:, :N_BIAS_A].reshape(N_BUCKETS, N_DIL_GROUPS, A_HEADS)
    rel_b = rel_bias[:, N_BIAS_A:N_BIAS_A + B_HEADS].reshape(N_BUCKETS, B_KV_GROUPS, B_HEADS_PER_GROUP)
    rel_c = rel_bias[:, N_BIAS_A + B_HEADS:]

    for l in range(DEPTH):
        lam_init = 0.8 - 0.6 * math.exp(-0.3 * l)
        h = rms_norm(x, norm_mix_pre[l])
        proj = h @ w_in[l]
        a_qkv, b_q, b_kv, b_gate, c_q, c_k, c_v, m_gate = jnp.split(proj, IN_OFFSETS, axis=-1)

        a = a_qkv.reshape(bsz, seq, 3, N_DIL_GROUPS, A_HEADS, HEAD_DIM)
        o_a = dilated_attention(a[:, :, 0], a[:, :, 1], a[:, :, 2], rel_a).reshape(bsz, seq, A_WIDTH)

        o_b = nsa_attention(b_q.reshape(bsz, seq, B_HEADS, HEAD_DIM),
                            b_kv.reshape(bsz, seq, 6, B_KV_GROUPS, HEAD_DIM),
                            jax.nn.sigmoid(b_gate.reshape(bsz, seq, B_HEADS, 3)),
                            cmp_k_pos[l], cmp_v_pos[l], cmp_k_w1[l], cmp_k_w2[l],
                            cmp_v_w1[l], cmp_v_w2[l], rel_b)

        o_c = diff_attention(c_q.reshape(bsz, seq, C_HEADS, 2, HEAD_DIM),
                             c_k.reshape(bsz, seq, C_HEADS, 2, HEAD_DIM),
                             c_v.reshape(bsz, seq, C_HEADS, C_VDIM),
                             diff_lambda[l], diff_norm[l], rel_c, lam_init)

        g = jax.nn.sigmoid(m_gate.reshape(bsz, seq, N_BRANCHES, D_MODEL))
        mixed = (g[:, :, 0] * (o_a @ w_branch_a[l]) + g[:, :, 1] * (o_b @ w_branch_b[l])
                 + g[:, :, 2] * (o_c @ w_branch_c[l]))
        x = x + rms_norm(mixed @ w_out[l], norm_mix_post[l])

        h = rms_norm(x, norm_mlp_pre[l])
        u = jnp.square(jax.nn.relu(h @ w_up[l]))
        x = x + rms_norm(u @ w_down[l], norm_mlp_post[l])
    return x
```

```python
import functools
import math

import jax
import jax.numpy as jnp
import numpy as np
from jax import lax
from jax.experimental import pallas as pl
from jax.experimental.pallas import tpu as pltpu

F32 = jnp.float32
BF16 = jnp.bfloat16

LANES = 128
HEAD_DIM = 64
SCALE = HEAD_DIM ** -0.5
NORM_EPS = 1e-6
NEG = -1e30
TINY = 1e-30
TQ = 128

DIL_GROUPS = ((128, 1), (512, 4), (2048, 16))
A_HEADS = 8
B_HEADS = 16
B_KV_GROUPS = 2
B_HPG = B_HEADS // B_KV_GROUPS
C_HEADS = 8
CMP_LEN, CMP_STRIDE, CMP_HIDDEN = 32, 16, 256
SEL_LEN, SEL_TOPK, SEL_LOCAL = 64, 16, 2
FORCED_SCORE = 1e6
WIN = 512
N_BUCKETS = 32
BIAS_MAX_DIST = 2048

U_MGATE = 0
U_A = 48
U_BQ = 84
U_BGATE0 = 92
U_BKV = 93
U_BGATE1 = 99
U_C = 100
U_TOTAL = 124
NCOL = U_TOTAL * LANES

VMEM_LIMIT = 48 * 1024 * 1024


def _cparams(sem):
    return pltpu.CompilerParams(dimension_semantics=sem, vmem_limit_bytes=VMEM_LIMIT)


def _rms(x, gain):
    return x * lax.rsqrt(jnp.mean(x * x, axis=-1, keepdims=True) + NORM_EPS) * gain


def _dot_nt(a, b):
    return lax.dot_general(a, b, (((1,), (1,)), ((), ())), preferred_element_type=F32)


def _dot(a, b):
    return jnp.dot(a, b, preferred_element_type=F32)


def _proj_kernel(x_ref, g_ref, w_ref, o_ref, h_ref):
    @pl.when(pl.program_id(1) == 0)
    def _():
        h_ref[...] = _rms(x_ref[...], g_ref[...]).astype(BF16)

    o_ref[...] = _dot(h_ref[...], w_ref[...]).astype(o_ref.dtype)


def _proj(x, gain, w, tm=1024, tn=512):
    n, d = x.shape
    return pl.pallas_call(
        _proj_kernel,
        out_shape=jax.ShapeDtypeStruct((n, w.shape[1]), BF16),
        grid=(n // tm, w.shape[1] // tn),
        in_specs=[pl.BlockSpec((tm, d), lambda i, j: (i, 0)),
                  pl.BlockSpec((1, d), lambda i, j: (0, 0)),
                  pl.BlockSpec((d, tn), lambda i, j: (0, j))],
        out_specs=pl.BlockSpec((tm, tn), lambda i, j: (i, j)),
        scratch_shapes=[pltpu.VMEM((tm, d), BF16)],
        compiler_params=_cparams(("parallel", "arbitrary")),
        name="proj",
    )(x, gain, w)


def _attn_a_kernel(q_ref, kc_ref, kp_ref, vc_ref, vp_ref, bias_ref, o_ref, lse_ref):
    g = pl.program_id(0)
    i = pl.program_id(1)
    period = jnp.where(g == 0, 16, jnp.where(g == 1, 4, 1))
    has_prev = lax.rem(i, period) != 0
    r = lax.broadcasted_iota(jnp.int32, (TQ, 2 * TQ), 0)
    c = lax.broadcasted_iota(jnp.int32, (TQ, 2 * TQ), 1)
    du = r + TQ - c
    valid = (du >= 0) & (du <= TQ) & ((c >= TQ) | has_prev)
    q = q_ref[...] * SCALE
    for h in range(A_HEADS):
        sl = slice(h * HEAD_DIM, (h + 1) * HEAD_DIM)
        qh = q[:, sl]
        s = jnp.concatenate([_dot_nt(qh, kp_ref[:, sl]), _dot_nt(qh, kc_ref[:, sl])], axis=1)
        s = jnp.where(valid, s + bias_ref[h], NEG)
        m = jnp.max(s, axis=-1, keepdims=True)
        e = jnp.where(valid, jnp.exp(s - m), 0.0)
        l = jnp.maximum(jnp.sum(e, axis=-1, keepdims=True), TINY)
        p = (e / l).astype(BF16)
        o = _dot(p[:, :TQ], vp_ref[:, sl]) + _dot(p[:, TQ:], vc_ref[:, sl])
        o_ref[:, sl] = o
        lse_ref[:, sl] = jnp.broadcast_to(m + jnp.log(l), (TQ, HEAD_DIM))


def _attn_a(qkv, bias):
    ng, n, _ = qkv.shape
    w = A_HEADS * HEAD_DIM
    cur = lambda col: pl.BlockSpec((None, TQ, w), lambda g, i: (g, i, col))
    prev = lambda col: pl.BlockSpec((None, TQ, w), lambda g, i: (g, jnp.maximum(i - 1, 0), col))
    out = pl.BlockSpec((None, TQ, w), lambda g, i: (g, i, 0))
    return pl.pallas_call(
        _attn_a_kernel,
        out_shape=(jax.ShapeDtypeStruct((ng, n, w), F32), jax.ShapeDtypeStruct((ng, n, w), F32)),
        grid=(ng, n // TQ),
        in_specs=[cur(0), cur(1), prev(1), cur(2), prev(2),
                  pl.BlockSpec((None, A_HEADS, TQ, 2 * TQ), lambda g, i: (g, 0, 0, 0))],
        out_specs=(out, out),
        compiler_params=_cparams(("parallel", "parallel")),
        name="attn_a",
    )(qkv, qkv, qkv, qkv, qkv, bias)


def _cmp_kernel(x_ref, pos_ref, w1_ref, w2_ref, o_ref):
    for kv in range(2):
        x = x_ref[kv].astype(F32)
        top = (x + pos_ref[kv, 0:1, :]).astype(BF16)
        bot = (x + pos_ref[kv, 1:2, :]).astype(BF16)
        p1 = _dot(top, w1_ref[kv, 0])
        p2 = _dot(bot, w1_ref[kv, 1])
        hid = p1 + pltpu.roll(p2, p2.shape[0] - 1, 0)
        o_ref[kv] = _dot(jax.nn.gelu(hid).astype(BF16), w2_ref[kv]).astype(o_ref.dtype)


def _cmp(xc, pos, w1, w2):
    nb = xc.shape[0]
    nch, cw = xc.shape[2], xc.shape[3]
    full = lambda a: pl.BlockSpec(a.shape, lambda i: (0,) * a.ndim)
    return pl.pallas_call(
        _cmp_kernel,
        out_shape=jax.ShapeDtypeStruct((nb, 2, nch, HEAD_DIM), BF16),
        grid=(nb,),
        in_specs=[pl.BlockSpec((None, 2, nch, cw), lambda i: (i, 0, 0, 0)),
                  full(pos), full(w1), full(w2)],
        out_specs=pl.BlockSpec((None, 2, nch, HEAD_DIM), lambda i: (i, 0, 0, 0)),
        compiler_params=_cparams(("parallel",)),
        name="cmp",
    )(xc, pos, w1, w2)


def _online_block(carry, s, mask, v):
    m, l, acc = carry
    s = jnp.where(mask, s, NEG)
    m_new = jnp.maximum(m, jnp.max(s, axis=-1, keepdims=True))
    alpha = jnp.exp(m - m_new)
    p = jnp.where(mask, jnp.exp(s - m_new), 0.0)
    l = alpha * l + jnp.sum(p, axis=-1, keepdims=True)
    acc = alpha * acc + _dot(p.astype(BF16), v)
    return m_new, l, acc


def _online_init(dv):
    return (jnp.full((TQ, 1), NEG, F32), jnp.zeros((TQ, 1), F32), jnp.zeros((TQ, dv), F32))


def _nsa_kernel(q_ref, kv_ref, gate_ref, kcvc_ref, bias_ref, e_ref, ov_ref, o_ref, selm_ref):
    i = pl.program_id(2)
    rows = lax.broadcasted_iota(jnp.int32, (TQ, 1), 0)
    t = i * TQ + rows
    q = q_ref[...] * SCALE
    gates = jax.nn.sigmoid(gate_ref[...].astype(F32))

    kc = kcvc_ref[0]
    vc = kcvc_ref[1]
    n_cmp = kc.shape[0]
    n_idx = lax.broadcasted_iota(jnp.int32, (1, n_cmp), 1)
    cmask = (n_idx * CMP_STRIDE + (CMP_LEN - 1)) <= t
    psum = jnp.zeros((TQ, n_cmp), F32)
    o_cmp = []
    for h in range(B_HPG):
        qh = q[:, h * HEAD_DIM:(h + 1) * HEAD_DIM]
        lc = jnp.where(cmask, _dot_nt(qh, kc), NEG)
        m = jnp.max(lc, axis=-1, keepdims=True)
        e = jnp.where(cmask, jnp.exp(lc - m), 0.0)
        p = e / jnp.maximum(jnp.sum(e, axis=-1, keepdims=True), TINY)
        psum = psum + p
        o_cmp.append(_dot(p.astype(BF16), vc))

    hi = psum.astype(BF16)
    lo = (psum - hi.astype(F32)).astype(BF16)
    score = _dot(hi, ov_ref[...]) + _dot(lo, ov_ref[...])
    n_sel = score.shape[1]
    jb = lax.broadcasted_iota(jnp.int32, (1, n_sel), 1)
    back = lax.shift_right_arithmetic(t, int(math.log2(SEL_LEN))) - jb
    forced = (jb == 0) | ((back >= 0) & (back < SEL_LOCAL))
    score = jnp.where(forced, FORCED_SCORE, score)
    score = jnp.where(jb * SEL_LEN <= t, score, NEG)
    rank = jnp.zeros((TQ, n_sel), jnp.int32)
    for ii in range(n_sel):
        col = score[:, ii:ii + 1]
        beats = (col > score) | ((col == score) & (ii < jb))
        rank = rank + beats.astype(jnp.int32)
    sel = (rank < SEL_TOPK).astype(BF16)
    selm = _dot(sel, e_ref[...])
    for jj in range(selm.shape[1] // TQ):
        selm_ref[jj] = selm[:, jj * TQ:(jj + 1) * TQ]

    rc = rows - lax.broadcasted_iota(jnp.int32, (1, TQ), 1)
    k_sel, v_sel, k_win, v_win = (2 * HEAD_DIM, 3 * HEAD_DIM, 4 * HEAD_DIM, 5 * HEAD_DIM)

    for h in range(B_HPG):
        qh = q[:, h * HEAD_DIM:(h + 1) * HEAD_DIM]

        def tile(j, kcol, vcol, qh=qh, h=h):
            ks = pl.multiple_of(j * TQ, TQ)
            kj = kv_ref[pl.ds(ks, TQ), kcol:kcol + HEAD_DIM]
            vj = kv_ref[pl.ds(ks, TQ), vcol:vcol + HEAD_DIM]
            return _dot_nt(qh, kj) + bias_ref[h, i - j], vj, rc + (i - j) * TQ

        def sel_body(j, carry):
            s, vj, dist = tile(j, k_sel, v_sel)
            return _online_block(carry, s, (selm_ref[j] > 0.5) & (dist >= 0), vj)

        def win_body(j, carry):
            s, vj, dist = tile(j, k_win, v_win)
            return _online_block(carry, s, (dist >= 0) & (dist < WIN), vj)

        _, l_s, acc_s = lax.fori_loop(0, i + 1, sel_body, _online_init(HEAD_DIM))
        _, l_w, acc_w = lax.fori_loop(jnp.maximum(i - WIN // TQ, 0), i + 1, win_body,
                                      _online_init(HEAD_DIM))
        o = (gates[:, h:h + 1] * o_cmp[h]
             + gates[:, B_HPG + h:B_HPG + h + 1] * (acc_s / l_s)
             + gates[:, 2 * B_HPG + h:2 * B_HPG + h + 1] * (acc_w / l_w))
        o_ref[:, h * HEAD_DIM:(h + 1) * HEAD_DIM] = o.astype(o_ref.dtype)


def _nsa(p3, kcvc, bias, e_mat, ov_mat):
    b, s, _ = p3.shape
    nq = s // TQ
    qw = B_HPG * HEAD_DIM
    kvw = 6 * HEAD_DIM
    full = lambda a: pl.BlockSpec(a.shape, lambda g, bb, i: (0,) * a.ndim)
    return pl.pallas_call(
        _nsa_kernel,
        out_shape=jax.ShapeDtypeStruct((b, s, B_HEADS * HEAD_DIM), BF16),
        grid=(B_KV_GROUPS, b, nq),
        in_specs=[
            pl.BlockSpec((None, TQ, qw), lambda g, bb, i: (bb, i, U_BQ * LANES // qw + g)),
            pl.BlockSpec((None, s, kvw), lambda g, bb, i: (bb, 0, U_BKV * LANES // kvw + g)),
            pl.BlockSpec((None, TQ, LANES),
                         lambda g, bb, i: (bb, i, U_BGATE0 + (U_BGATE1 - U_BGATE0) * g)),
            pl.BlockSpec((None, 2, kcvc.shape[2], HEAD_DIM),
                         lambda g, bb, i: (bb * B_KV_GROUPS + g, 0, 0, 0)),
            pl.BlockSpec((B_HPG, nq, TQ, TQ), lambda g, bb, i: (g, 0, 0, 0)),
            full(e_mat), full(ov_mat)],
        out_specs=pl.BlockSpec((None, TQ, qw), lambda g, bb, i: (bb, i, g)),
        scratch_shapes=[pltpu.VMEM((nq, TQ, TQ), F32)],
        compiler_params=_cparams(("parallel", "parallel", "arbitrary")),
        name="nsa",
    )(p3, p3, p3, kcvc, bias, e_mat, ov_mat)


def _diff_kernel(q_ref, k_ref, v_ref, bias_ref, lam_ref, gain_ref, o_ref, *, lam_init):
    i = pl.program_id(2)
    lv = lam_ref[...]
    lam = (jnp.exp(jnp.sum(lv[0:1] * lv[1:2], axis=-1, keepdims=True))
           - jnp.exp(jnp.sum(lv[2:3] * lv[3:4], axis=-1, keepdims=True)) + lam_init)
    q = q_ref[...] * SCALE
    q1, q2 = q[:, :HEAD_DIM], q[:, HEAD_DIM:]
    rc = (lax.broadcasted_iota(jnp.int32, (TQ, 1), 0)
          - lax.broadcasted_iota(jnp.int32, (1, TQ), 1))
    dv = v_ref.shape[1]

    def body(j, carry):
        c1, c2 = carry
        ks = pl.multiple_of(j * TQ, TQ)
        kj = k_ref[pl.ds(ks, TQ), :]
        vj = v_ref[pl.ds(ks, TQ), :]
        bias = bias_ref[i - j]
        mask = rc + (i - j) * TQ >= 0
        c1 = _online_block(c1, _dot_nt(q1, kj[:, :HEAD_DIM]) + bias, mask, vj)
        c2 = _online_block(c2, _dot_nt(q2, kj[:, HEAD_DIM:]) + bias, mask, vj)
        return c1, c2

    (_, l1, a1), (_, l2, a2) = lax.fori_loop(0, i + 1, body, (_online_init(dv), _online_init(dv)))
    o = a1 / l1 - lam * (a2 / l2)
    o_ref[...] = (_rms(o, gain_ref[...]) * (1.0 - lam_init)).astype(o_ref.dtype)


def _diff(p3, bias, lam_vecs, sub_gain, lam_init):
    b, s, _ = p3.shape
    nq = s // TQ
    dv = 2 * HEAD_DIM
    full = lambda a: pl.BlockSpec(a.shape, lambda h, bb, i: (0,) * a.ndim)
    return pl.pallas_call(
        functools.partial(_diff_kernel, lam_init=lam_init),
        out_shape=jax.ShapeDtypeStruct((b, s, C_HEADS * dv), BF16),
        grid=(C_HEADS, b, nq),
        in_specs=[
            pl.BlockSpec((None, TQ, dv), lambda h, bb, i: (bb, i, U_C + h)),
            pl.BlockSpec((None, s, dv), lambda h, bb, i: (bb, 0, U_C + C_HEADS + h)),
            pl.BlockSpec((None, s, dv), lambda h, bb, i: (bb, 0, U_C + 2 * C_HEADS + h)),
            pl.BlockSpec((None, nq, TQ, TQ), lambda h, bb, i: (h, 0, 0, 0)),
            full(lam_vecs), full(sub_gain)],
        out_specs=pl.BlockSpec((None, TQ, dv), lambda h, bb, i: (bb, i, h)),
        compiler_params=_cparams(("parallel", "parallel", "arbitrary")),
        name="diff",
    )(p3, p3, p3, bias, lam_vecs, sub_gain)


def _mixout_kernel(x_ref, oa_ref, lse_ref, ob_ref, oc_ref, g0_ref, g1_ref, g2_ref,
                   wa_ref, wb_ref, wc_ref, wo_ref, gain_ref, o_ref):
    lse = lse_ref[...]
    e = jnp.exp(lse - jnp.max(lse, axis=0, keepdims=True))
    wts = e / jnp.sum(e, axis=0, keepdims=True)
    oa = wts[0] * oa_ref[0] + wts[1] * oa_ref[1] + wts[2] * oa_ref[2]
    mixed = (jax.nn.sigmoid(g0_ref[...].astype(F32)) * _dot(oa.astype(BF16), wa_ref[...])
             + jax.nn.sigmoid(g1_ref[...].astype(F32)) * _dot(ob_ref[...], wb_ref[...])
             + jax.nn.sigmoid(g2_ref[...].astype(F32)) * _dot(oc_ref[...], wc_ref[...]))
    y = _dot(mixed.astype(BF16), wo_ref[...])
    o_ref[...] = x_ref[...] + _rms(y, gain_ref[...])


def _mixout(x, oa, lse, ob, oc, p, wa, wb, wc, wo, gain, tm=256):
    n, d = x.shape
    const = lambda a: pl.BlockSpec(a.shape, lambda i: (0,) * a.ndim, pipeline_mode=pl.Buffered(1))
    row = lambda a: pl.BlockSpec((tm, a.shape[1]), lambda i: (i, 0))
    grp = lambda a: pl.BlockSpec((a.shape[0], tm, a.shape[2]), lambda i: (0, i, 0))
    gate = lambda br: pl.BlockSpec((tm, d), lambda i: (i, U_MGATE * LANES // d + br))
    return pl.pallas_call(
        _mixout_kernel,
        out_shape=jax.ShapeDtypeStruct((n, d), F32),
        grid=(n // tm,),
        in_specs=[row(x), grp(oa), grp(lse), row(ob), row(oc), gate(0), gate(1), gate(2),
                  const(wa), const(wb), const(wc), const(wo), const(gain)],
        out_specs=row(x),
        compiler_params=_cparams(("parallel",)),
        name="mixout",
    )(x, oa, lse, ob, oc, p, p, p, wa, wb, wc, wo, gain)


def _mlp_kernel(x_ref, gpre_ref, wup_ref, wdn_ref, gpost_ref, o_ref, h_ref, acc_ref):
    f = pl.program_id(1)

    @pl.when(f == 0)
    def _():
        h_ref[...] = _rms(x_ref[...], gpre_ref[...]).astype(BF16)
        acc_ref[...] = jnp.zeros_like(acc_ref)

    u = jnp.square(jax.nn.relu(_dot(h_ref[...], wup_ref[...])))
    acc_ref[...] += _dot(u.astype(BF16), wdn_ref[...])

    @pl.when(f == pl.num_programs(1) - 1)
    def _():
        o_ref[...] = x_ref[...] + _rms(acc_ref[...], gpost_ref[...])


def _mlp(x, gpre, wup, wdn, gpost, tm=512, tf=1024):
    n, d = x.shape
    dff = wup.shape[1]
    return pl.pallas_call(
        _mlp_kernel,
        out_shape=jax.ShapeDtypeStruct((n, d), F32),
        grid=(n // tm, dff // tf),
        in_specs=[pl.BlockSpec((tm, d), lambda i, f: (i, 0)),
                  pl.BlockSpec((1, d), lambda i, f: (0, 0)),
                  pl.BlockSpec((d, tf), lambda i, f: (0, f)),
                  pl.BlockSpec((tf, d), lambda i, f: (f, 0)),
                  pl.BlockSpec((1, d), lambda i, f: (0, 0))],
        out_specs=pl.BlockSpec((tm, d), lambda i, f: (i, 0)),
        scratch_shapes=[pltpu.VMEM((tm, d), BF16), pltpu.VMEM((tm, d), F32)],
        compiler_params=_cparams(("parallel", "arbitrary")),
        name="mlp",
    )(x, gpre, wup, wdn, gpost)


def _t5_bucket(dist):
    exact = N_BUCKETS // 2
    d = jnp.maximum(dist, 0)
    logd = jnp.log(jnp.maximum(d, 1).astype(F32) / exact)
    far = exact + (logd / math.log(BIAS_MAX_DIST / exact) * (N_BUCKETS - exact)).astype(jnp.int32)
    return jnp.where(d < exact, d, jnp.minimum(far, N_BUCKETS - 1))


def _bias_tables(rel_bias, seq):
    per_dist = rel_bias[_t5_bucket(jnp.arange(seq))].T
    n_a = len(DIL_GROUPS) * A_HEADS
    r = np.arange(TQ)[:, None]
    c2 = np.arange(2 * TQ)[None, :]
    bias_a = []
    for g, (_, dil) in enumerate(DIL_GROUPS):
        idx = np.clip(dil * (r + TQ - c2), 0, seq - 1)
        bias_a.append(per_dist[g * A_HEADS:(g + 1) * A_HEADS][:, idx])
    bias_a = jnp.stack(bias_a)
    c = np.arange(TQ)[None, :]
    idx = np.clip(np.arange(seq // TQ)[:, None, None] * TQ + r[None] - c[None], 0, seq - 1)
    bias_b = per_dist[n_a:n_a + B_HEADS][:, idx]
    bias_c = per_dist[n_a + B_HEADS:][:, idx]
    return bias_a, bias_b, bias_c


def _pack_w_in(w):
    d = w.shape[0]
    a_w = 3 * len(DIL_GROUPS) * A_HEADS * HEAD_DIM
    bq_w = B_HEADS * HEAD_DIM
    bkv_w = 6 * B_KV_GROUPS * HEAD_DIM
    bg_w = 3 * B_HEADS
    c_w = 3 * C_HEADS * 2 * HEAD_DIM
    o = 0
    a = w[:, o:o + a_w]; o += a_w
    bq = w[:, o:o + bq_w]; o += bq_w
    bkv = w[:, o:o + bkv_w]; o += bkv_w
    bg = w[:, o:o + bg_w]; o += bg_w
    cc = w[:, o:o + c_w]; o += c_w
    mg = w[:, o:]
    bkv = bkv.reshape(d, 6, B_KV_GROUPS, HEAD_DIM).transpose(0, 2, 1, 3).reshape(d, bkv_w)
    bg = bg.reshape(d, B_KV_GROUPS, B_HPG, 3).transpose(0, 1, 3, 2).reshape(d, B_KV_GROUPS, 3 * B_HPG)
    bg = jnp.pad(bg, ((0, 0), (0, 0), (0, LANES - 3 * B_HPG)))
    packed = jnp.concatenate([mg, a, bq, bg[:, 0], bkv, bg[:, 1], cc], axis=1)
    assert packed.shape[1] == NCOL
    return packed.astype(BF16)


def _to_residue_major(a, bsz, seq, dil):
    w = a.shape[-1]
    return a.reshape(bsz, seq // dil, dil, w).transpose(0, 2, 1, 3).reshape(bsz * seq, w)


def _from_residue_major(a, bsz, seq, dil):
    w = a.shape[-1]
    return a.reshape(bsz, dil, seq // dil, w).transpose(0, 2, 1, 3).reshape(bsz * seq, w)


def kernel(x, rel_bias, w_in, cmp_k_pos, cmp_v_pos, cmp_k_w1, cmp_k_w2, cmp_v_w1, cmp_v_w2, diff_lambda, diff_norm, w_branch_a, w_branch_b, w_branch_c, w_out, norm_mix_pre, norm_mix_post, norm_mlp_pre, norm_mlp_post, w_up, w_down):
    bsz, seq, d = x.shape
    n = bsz * seq
    depth = w_in.shape[0]
    n_grp = len(DIL_GROUPS)
    aw = A_HEADS * HEAD_DIM
    nch = seq // CMP_STRIDE

    bias_a, bias_b, bias_c = _bias_tables(rel_bias, seq)
    e_mat = jnp.asarray(np.arange(seq)[None, :] // SEL_LEN == np.arange(seq // SEL_LEN)[:, None], BF16)
    starts = np.arange(nch)[:, None] * CMP_STRIDE
    bstart = np.arange(seq // SEL_LEN)[None, :] * SEL_LEN
    ov_mat = jnp.asarray((starts < bstart + SEL_LEN) & (starts + CMP_LEN > bstart), BF16)

    xf = x.reshape(n, d)
    for l in range(depth):
        lam_init = 0.8 - 0.6 * math.exp(-0.3 * l)
        p = _proj(xf, norm_mix_pre[l][None], _pack_w_in(w_in[l]))
        p3 = p.reshape(bsz, seq, NCOL)

        a = p[:, U_A * LANES:(U_A + 3 * n_grp * aw // LANES) * LANES].reshape(n, 3, n_grp, aw)
        qkv = jnp.stack([_to_residue_major(a[:, :, g].reshape(n, 3 * aw), bsz, seq, dil)
                         for g, (_, dil) in enumerate(DIL_GROUPS)])
        oa, lse = _attn_a(qkv, bias_a)
        oa = jnp.stack([_from_residue_major(oa[g], bsz, seq, dil) for g, (_, dil) in enumerate(DIL_GROUPS)])
        lse = jnp.stack([_from_residue_major(lse[g], bsz, seq, dil) for g, (_, dil) in enumerate(DIL_GROUPS)])

        kvc = p3[:, :, U_BKV * LANES:(U_BKV + 6) * LANES].reshape(bsz, seq, B_KV_GROUPS, 6, HEAD_DIM)
        xc = kvc[:, :, :, 0:2].transpose(0, 2, 3, 1, 4).reshape(bsz * B_KV_GROUPS, 2, nch, CMP_STRIDE * HEAD_DIM)
        half = CMP_STRIDE * HEAD_DIM
        pos = jnp.stack([cmp_k_pos[l], cmp_v_pos[l]]).reshape(2, 2, half)
        w1 = jnp.stack([cmp_k_w1[l], cmp_v_w1[l]]).reshape(2, 2, half, CMP_HIDDEN).astype(BF16)
        w2 = jnp.stack([cmp_k_w2[l], cmp_v_w2[l]]).astype(BF16)
        kcvc = _cmp(xc, pos, w1, w2)
        ob = _nsa(p3, kcvc, bias_b, e_mat, ov_mat).reshape(n, B_HEADS * HEAD_DIM)

        oc = _diff(p3, bias_c, diff_lambda[l], diff_norm[l][None], lam_init).reshape(n, C_HEADS * 2 * HEAD_DIM)

        xf = _mixout(xf, oa, lse, ob, oc, p,
                     w_branch_a[l].astype(BF16), w_branch_b[l].astype(BF16), w_branch_c[l].astype(BF16),
                     w_out[l].astype(BF16), norm_mix_post[l][None])
        xf = _mlp(xf, norm_mlp_pre[l][None], w_up[l].astype(BF16), w_down[l].astype(BF16),
                  norm_mlp_post[l][None])
    return xf.reshape(bsz, seq, d)
```

```python
import functools
import math

import jax
import jax.numpy as jnp
import numpy as np
from jax import lax
from jax.experimental import pallas as pl
from jax.experimental.pallas import tpu as pltpu

F32 = jnp.float32
BF16 = jnp.bfloat16

LANES = 128
HEAD_DIM = 64
SCALE = HEAD_DIM ** -0.5
NORM_EPS = 1e-6
NEG = -1e30
TINY = 1e-30
TQ = 128
TQC = 256
EXT_STEP = 512

DIL_GROUPS = ((128, 1), (512, 4), (2048, 16))
A_HEADS = 8
B_HEADS = 16
B_KV_GROUPS = 2
B_HPG = B_HEADS // B_KV_GROUPS
C_HEADS = 8
CMP_LEN, CMP_STRIDE, CMP_HIDDEN = 32, 16, 256
SEL_LEN, SEL_TOPK, SEL_LOCAL = 64, 16, 2
FORCED_SCORE = 1e6
WIN = 512
N_BUCKETS = 32
BIAS_MAX_DIST = 2048

U_BKV = 0
BKV_UNITS = 5
U_BGATE = 10
U_BQ = 12
U_A = 20
U_C = 56
U_MGATE = 80
U_TOTAL = 128
NCOL = U_TOTAL * LANES

VMEM_LIMIT = 56 * 1024 * 1024


def _cparams(sem):
    return pltpu.CompilerParams(dimension_semantics=sem, vmem_limit_bytes=VMEM_LIMIT)


def _rms(x, gain):
    return x * lax.rsqrt(jnp.mean(x * x, axis=-1, keepdims=True) + NORM_EPS) * gain


def _dot_nt(a, b):
    return lax.dot_general(a, b, (((1,), (1,)), ((), ())), preferred_element_type=F32)


def _dot(a, b):
    return jnp.dot(a, b, preferred_element_type=F32)


def _lane_half(shape):
    return lax.broadcasted_iota(jnp.int32, shape, len(shape) - 1) < HEAD_DIM


def _stack_halves(tiles):
    low = _lane_half(tiles[0].shape)
    out = []
    for t in tiles:
        z = jnp.zeros_like(t)
        out += [jnp.where(low, t, z), jnp.where(low, z, t)]
    return jnp.concatenate(out, axis=0)


def _softmax_pv(s, v):
    m = jnp.max(s, axis=-1, keepdims=True)
    p = jnp.exp(s - m)
    l = jnp.sum(p, axis=-1, keepdims=True)
    return _dot(p.astype(BF16), v) / l, m, l


def _proj_kernel(x_ref, g_ref, w_ref, o_ref, h_ref):
    @pl.when(pl.program_id(1) == 0)
    def _():
        h_ref[...] = _rms(x_ref[...], g_ref[...]).astype(BF16)

    o_ref[...] = _dot(h_ref[...], w_ref[...]).astype(o_ref.dtype)


def _proj(x, gain, w, tm=1024, tn=512):
    n, d = x.shape
    return pl.pallas_call(
        _proj_kernel,
        out_shape=jax.ShapeDtypeStruct((n, w.shape[1]), BF16),
        grid=(n // tm, w.shape[1] // tn),
        in_specs=[pl.BlockSpec((tm, d), lambda i, j: (i, 0)),
                  pl.BlockSpec((1, d), lambda i, j: (0, 0)),
                  pl.BlockSpec((d, tn), lambda i, j: (0, j))],
        out_specs=pl.BlockSpec((tm, tn), lambda i, j: (i, j)),
        scratch_shapes=[pltpu.VMEM((tm, d), BF16)],
        compiler_params=_cparams(("parallel", "arbitrary")),
        name="proj",
    )(x, gain, w)


def _attn_a_kernel(q_ref, kc_ref, kp_ref, vc_ref, vp_ref, bias_ref, o_ref, lse_ref):
    g = pl.program_id(0)
    i = pl.program_id(1)
    period = jnp.where(g == 0, 16, jnp.where(g == 1, 4, 1))
    has_prev = (lax.rem(i, period) != 0).astype(jnp.int32)
    q = q_ref[...] * SCALE
    n_pair = A_HEADS // 2
    pair = lambda ref, p: ref[:, p * LANES:(p + 1) * LANES]
    s = []
    for p in range(n_pair):
        qs = _stack_halves([pair(q, p)])
        ks = jnp.concatenate([pair(kp_ref, p), pair(kc_ref, p)], axis=0)
        s.append(_dot_nt(qs, ks))
    s = jnp.concatenate(s, axis=0) + bias_ref[has_prev]
    m = jnp.max(s, axis=-1, keepdims=True)
    e = jnp.exp(s - m)
    l = jnp.sum(e, axis=-1, keepdims=True)
    pb = e.astype(BF16)
    lse = m + jnp.log(l)
    low = _lane_half((TQ, LANES))
    for p in range(n_pair):
        r0 = 2 * p * TQ
        vs = jnp.concatenate([pair(vp_ref, p), pair(vc_ref, p)], axis=0)
        o = _dot(pb[r0:r0 + 2 * TQ], vs) / l[r0:r0 + 2 * TQ]
        o_ref[:, p * LANES:(p + 1) * LANES] = jnp.where(low, o[:TQ], o[TQ:])
        lse_ref[:, p * LANES:(p + 1) * LANES] = jnp.where(low, lse[r0:r0 + TQ], lse[r0 + TQ:r0 + 2 * TQ])


def _attn_a(qkv, bias):
    ng, n, _ = qkv.shape
    w = A_HEADS * HEAD_DIM
    cur = lambda col: pl.BlockSpec((None, TQ, w), lambda g, i: (g, i, col))
    prev = lambda col: pl.BlockSpec((None, TQ, w), lambda g, i: (g, jnp.maximum(i - 1, 0), col))
    out = pl.BlockSpec((None, TQ, w), lambda g, i: (g, i, 0))
    return pl.pallas_call(
        _attn_a_kernel,
        out_shape=(jax.ShapeDtypeStruct((ng, n, w), F32), jax.ShapeDtypeStruct((ng, n, w), F32)),
        grid=(ng, n // TQ),
        in_specs=[cur(0), cur(1), prev(1), cur(2), prev(2),
                  pl.BlockSpec((None,) + bias.shape[1:], lambda g, i: (g, 0, 0, 0))],
        out_specs=(out, out),
        compiler_params=_cparams(("parallel", "parallel")),
        name="attn_a",
    )(qkv, qkv, qkv, qkv, qkv, bias)


def _cmp_kernel(x_ref, pos_ref, w1_ref, w2_ref, o_ref):
    for kv in range(2):
        x = x_ref[kv].astype(F32)
        top = (x + pos_ref[kv, 0:1, :]).astype(BF16)
        bot = (x + pos_ref[kv, 1:2, :]).astype(BF16)
        p1 = _dot(top, w1_ref[kv, 0])
        p2 = _dot(bot, w1_ref[kv, 1])
        hid = p1 + pltpu.roll(p2, p2.shape[0] - 1, 0)
        out = _dot(jax.nn.gelu(hid).astype(BF16), w2_ref[kv]).astype(o_ref.dtype)
        o_ref[kv] = jnp.concatenate([out, out], axis=1)


def _cmp(xc, pos, w1, w2):
    nb = xc.shape[0]
    nch, cw = xc.shape[2], xc.shape[3]
    full = lambda a: pl.BlockSpec(a.shape, lambda i: (0,) * a.ndim)
    return pl.pallas_call(
        _cmp_kernel,
        out_shape=jax.ShapeDtypeStruct((nb, 2, nch, LANES), BF16),
        grid=(nb,),
        in_specs=[pl.BlockSpec((None, 2, nch, cw), lambda i: (i, 0, 0, 0)),
                  full(pos), full(w1), full(w2)],
        out_specs=pl.BlockSpec((None, 2, nch, LANES), lambda i: (i, 0, 0, 0)),
        compiler_params=_cparams(("parallel",)),
        name="cmp",
    )(xc, pos, w1, w2)


N_SEL_TILES = 20
N_WIN_TILES = 9


def _nsa_kernel(q_ref, kv_ref, gate_ref, kcvc_ref, bsel_ref, bwin_ref, e_ref, ov_ref, o_ref):
    i = pl.program_id(2)
    seq = kv_ref.shape[0]
    rows = lax.broadcasted_iota(jnp.int32, (TQ, 1), 0)
    t = i * TQ + rows
    q = q_ref[...] * SCALE
    qs = _stack_halves([q[:, p * LANES:(p + 1) * LANES] for p in range(B_HPG // 2)])
    gates = jax.nn.sigmoid(gate_ref[...].astype(F32))
    head = lambda a, h: a[h * TQ:(h + 1) * TQ]

    kc = kcvc_ref[0]
    vc = kcvc_ref[1]
    n_cmp = kc.shape[0]
    n_idx = lax.broadcasted_iota(jnp.int32, (1, n_cmp), 1)
    cmask = (n_idx * CMP_STRIDE + (CMP_LEN - 1)) <= t
    c_add = jnp.concatenate([jnp.where(cmask, 0.0, NEG)] * B_HPG, axis=0)
    c_mul = jnp.concatenate([cmask.astype(F32)] * B_HPG, axis=0)
    lc = _dot_nt(qs, kc) + c_add
    e = jnp.exp(lc - jnp.max(lc, axis=-1, keepdims=True)) * c_mul
    p_cmp = e / jnp.maximum(jnp.sum(e, axis=-1, keepdims=True), TINY)
    o_cmp = _dot(p_cmp.astype(BF16), vc)
    psum = head(p_cmp, 0)
    for h in range(1, B_HPG):
        psum = psum + head(p_cmp, h)

    hi = psum.astype(BF16)
    lo = (psum - hi.astype(F32)).astype(BF16)
    score = _dot(hi, ov_ref[...]) + _dot(lo, ov_ref[...])
    n_sel = score.shape[1]
    jb = lax.broadcasted_iota(jnp.int32, (1, n_sel), 1)
    back = lax.shift_right_arithmetic(t, int(math.log2(SEL_LEN))) - jb
    forced = (jb == 0) | ((back >= 0) & (back < SEL_LOCAL))
    score = jnp.where(forced, FORCED_SCORE, score)
    score = jnp.where(jb * SEL_LEN <= t, score, NEG)
    rank = jnp.zeros((TQ, n_sel), jnp.int32)
    for ii in range(n_sel):
        col = score[:, ii:ii + 1]
        beats = (col > score) | ((col == score) & (ii < jb))
        rank = rank + beats.astype(jnp.int32)
    sel = (rank < SEL_TOPK).astype(BF16)
    sel_add = jnp.where(_dot(sel, e_ref[...]) > 0.5, 0.0, NEG)

    n_wt = WIN // TQ + 1
    kt0 = jnp.maximum(i - WIN // TQ, 0)
    k0 = pl.multiple_of(kt0 * TQ, TQ)
    s_w = _dot_nt(qs, kv_ref[pl.ds(k0, n_wt * TQ), LANES:2 * LANES])
    w0 = WIN // TQ - i + kt0
    b_w = jnp.concatenate(
        [jnp.concatenate([bwin_ref[h, w0 + kk] for kk in range(n_wt)], axis=1) for h in range(B_HPG)], axis=0)
    o_win, _, _ = _softmax_pv(s_w + b_w, kv_ref[pl.ds(k0, n_wt * TQ), 3 * LANES:4 * LANES])

    def finish(o_sel):
        low = _lane_half((TQ, LANES))
        outs = []
        for h in range(B_HPG):
            outs.append(gates[:, h:h + 1] * head(o_cmp, h)
                        + gates[:, B_HPG + h:B_HPG + h + 1] * head(o_sel, h)
                        + gates[:, 2 * B_HPG + h:2 * B_HPG + h + 1] * head(o_win, h))
        for p in range(B_HPG // 2):
            o_ref[:, p * LANES:(p + 1) * LANES] = jnp.where(low, outs[2 * p], outs[2 * p + 1]).astype(o_ref.dtype)

    tiles_per_ext = EXT_STEP // TQ
    for v in range(seq // EXT_STEP):
        @pl.when(i // tiles_per_ext == v)
        def _(v=v):
            ext = (v + 1) * EXT_STEP
            s_s = _dot_nt(qs, kv_ref[0:ext, 0:LANES])
            b0 = (N_SEL_TILES - tiles_per_ext - 1) - i
            b_s = jnp.concatenate(
                [jnp.concatenate([bsel_ref[h, b0 + kk] for kk in range(ext // TQ)], axis=1) + sel_add[:, :ext]
                 for h in range(B_HPG)], axis=0)
            o_sel, _, _ = _softmax_pv(s_s + b_s, kv_ref[0:ext, 2 * LANES:3 * LANES])
            finish(o_sel)


def _nsa(p3, kcvc, bsel, bwin, e_mat, ov_mat):
    b, s, _ = p3.shape
    nq = s // TQ
    qw = B_HPG * HEAD_DIM
    kvw = BKV_UNITS * LANES
    full = lambda a: pl.BlockSpec(a.shape, lambda g, bb, i: (0,) * a.ndim)
    per_group = lambda a: pl.BlockSpec((B_HPG,) + a.shape[1:], lambda g, bb, i: (g, 0, 0, 0),
                                       pipeline_mode=pl.Buffered(1))
    return pl.pallas_call(
        _nsa_kernel,
        out_shape=jax.ShapeDtypeStruct((b, s, B_HEADS * HEAD_DIM), BF16),
        grid=(B_KV_GROUPS, b, nq),
        in_specs=[
            pl.BlockSpec((None, TQ, qw), lambda g, bb, i: (bb, i, U_BQ * LANES // qw + g)),
            pl.BlockSpec((None, s, kvw), lambda g, bb, i: (bb, 0, U_BKV * LANES // kvw + g)),
            pl.BlockSpec((None, TQ, LANES), lambda g, bb, i: (bb, i, U_BGATE + g)),
            pl.BlockSpec((None, 2, kcvc.shape[2], LANES), lambda g, bb, i: (bb * B_KV_GROUPS + g, 0, 0, 0)),
            per_group(bsel), per_group(bwin), full(e_mat), full(ov_mat)],
        out_specs=pl.BlockSpec((None, TQ, qw), lambda g, bb, i: (bb, i, g)),
        compiler_params=_cparams(("parallel", "parallel", "arbitrary")),
        name="nsa",
    )(p3, p3, p3, kcvc, bsel, bwin, e_mat, ov_mat)


N_C_TILES = 18


def _diff_kernel(q_ref, k_ref, v_ref, bias_ref, lam_ref, gain_ref, o_ref, *, lam_init):
    i = pl.program_id(2)
    seq = k_ref.shape[0]
    lv = lam_ref[...]
    lam = (jnp.exp(jnp.sum(lv[0:1] * lv[1:2], axis=-1, keepdims=True))
           - jnp.exp(jnp.sum(lv[2:3] * lv[3:4], axis=-1, keepdims=True)) + lam_init)
    qs = _stack_halves([q_ref[...] * SCALE])
    q_per_ext = EXT_STEP // TQC
    for v in range(seq // EXT_STEP):
        @pl.when(i // q_per_ext == v)
        def _(v=v):
            ext = (v + 1) * EXT_STEP
            b0 = (N_C_TILES - EXT_STEP // TQ) - 2 * i
            bias = jnp.concatenate([bias_ref[b0 + kk] for kk in range(ext // TQ)], axis=1)
            s = _dot_nt(qs, k_ref[0:ext, :]) + jnp.concatenate([bias, bias], axis=0)
            o, _, _ = _softmax_pv(s, v_ref[0:ext, :])
            o = o[:TQC] - lam * o[TQC:]
            o_ref[...] = (_rms(o, gain_ref[...]) * (1.0 - lam_init)).astype(o_ref.dtype)


def _diff(p3, bias, lam_vecs, sub_gain, lam_init):
    b, s, _ = p3.shape
    dv = 2 * HEAD_DIM
    full = lambda a: pl.BlockSpec(a.shape, lambda h, bb, i: (0,) * a.ndim)
    return pl.pallas_call(
        functools.partial(_diff_kernel, lam_init=lam_init),
        out_shape=jax.ShapeDtypeStruct((b, s, C_HEADS * dv), BF16),
        grid=(C_HEADS, b, s // TQC),
        in_specs=[
            pl.BlockSpec((None, TQC, dv), lambda h, bb, i: (bb, i, U_C + h)),
            pl.BlockSpec((None, s, dv), lambda h, bb, i: (bb, 0, U_C + C_HEADS + h)),
            pl.BlockSpec((None, s, dv), lambda h, bb, i: (bb, 0, U_C + 2 * C_HEADS + h)),
            pl.BlockSpec((None,) + bias.shape[1:], lambda h, bb, i: (h, 0, 0, 0)),
            full(lam_vecs), full(sub_gain)],
        out_specs=pl.BlockSpec((None, TQC, dv), lambda h, bb, i: (bb, i, h)),
        compiler_params=_cparams(("parallel", "parallel", "arbitrary")),
        name="diff",
    )(p3, p3, p3, bias, lam_vecs, sub_gain)


def _mixout_kernel(x_ref, oa_ref, lse_ref, ob_ref, oc_ref, g0_ref, g1_ref, g2_ref,
                   wa_ref, wb_ref, wc_ref, wo_ref, gain_ref, o_ref):
    lse = lse_ref[...]
    e = jnp.exp(lse - jnp.max(lse, axis=0, keepdims=True))
    wts = e / jnp.sum(e, axis=0, keepdims=True)
    oa = wts[0] * oa_ref[0] + wts[1] * oa_ref[1] + wts[2] * oa_ref[2]
    mixed = (jax.nn.sigmoid(g0_ref[...].astype(F32)) * _dot(oa.astype(BF16), wa_ref[...])
             + jax.nn.sigmoid(g1_ref[...].astype(F32)) * _dot(ob_ref[...], wb_ref[...])
             + jax.nn.sigmoid(g2_ref[...].astype(F32)) * _dot(oc_ref[...], wc_ref[...]))
    y = _dot(mixed.astype(BF16), wo_ref[...])
    o_ref[...] = x_ref[...] + _rms(y, gain_ref[...])


def _mixout(x, oa, lse, ob, oc, p, wa, wb, wc, wo, gain, tm=256):
    n, d = x.shape
    const = lambda a: pl.BlockSpec(a.shape, lambda i: (0,) * a.ndim, pipeline_mode=pl.Buffered(1))
    row = lambda a: pl.BlockSpec((tm, a.shape[1]), lambda i: (i, 0))
    grp = lambda a: pl.BlockSpec((a.shape[0], tm, a.shape[2]), lambda i: (0, i, 0))
    gate = lambda br: pl.BlockSpec((tm, d), lambda i: (i, U_MGATE * LANES // d + br))
    return pl.pallas_call(
        _mixout_kernel,
        out_shape=jax.ShapeDtypeStruct((n, d), F32),
        grid=(n // tm,),
        in_specs=[row(x), grp(oa), grp(lse), row(ob), row(oc), gate(0), gate(1), gate(2),
                  const(wa), const(wb), const(wc), const(wo), const(gain)],
        out_specs=row(x),
        compiler_params=_cparams(("parallel",)),
        name="mixout",
    )(x, oa, lse, ob, oc, p, p, p, wa, wb, wc, wo, gain)


def _mlp_kernel(x_ref, gpre_ref, wup_ref, wdn_ref, gpost_ref, o_ref, h_ref, acc_ref):
    f = pl.program_id(1)

    @pl.when(f == 0)
    def _():
        h_ref[...] = _rms(x_ref[...], gpre_ref[...]).astype(BF16)
        acc_ref[...] = jnp.zeros_like(acc_ref)

    u = jnp.square(jax.nn.relu(_dot(h_ref[...], wup_ref[...])))
    acc_ref[...] += _dot(u.astype(BF16), wdn_ref[...])

    @pl.when(f == pl.num_programs(1) - 1)
    def _():
        o_ref[...] = x_ref[...] + _rms(acc_ref[...], gpost_ref[...])


def _mlp(x, gpre, wup, wdn, gpost, tm=512, tf=1024):
    n, d = x.shape
    dff = wup.shape[1]
    return pl.pallas_call(
        _mlp_kernel,
        out_shape=jax.ShapeDtypeStruct((n, d), F32),
        grid=(n // tm, dff // tf),
        in_specs=[pl.BlockSpec((tm, d), lambda i, f: (i, 0)),
                  pl.BlockSpec((1, d), lambda i, f: (0, 0)),
                  pl.BlockSpec((d, tf), lambda i, f: (0, f)),
                  pl.BlockSpec((tf, d), lambda i, f: (f, 0)),
                  pl.BlockSpec((1, d), lambda i, f: (0, 0))],
        out_specs=pl.BlockSpec((tm, d), lambda i, f: (i, 0)),
        scratch_shapes=[pltpu.VMEM((tm, d), BF16), pltpu.VMEM((tm, d), F32)],
        compiler_params=_cparams(("parallel", "arbitrary")),
        name="mlp",
    )(x, gpre, wup, wdn, gpost)


def _t5_bucket(dist):
    exact = N_BUCKETS // 2
    d = jnp.maximum(dist, 0)
    logd = jnp.log(jnp.maximum(d, 1).astype(F32) / exact)
    far = exact + (logd / math.log(BIAS_MAX_DIST / exact) * (N_BUCKETS - exact)).astype(jnp.int32)
    return jnp.where(d < exact, d, jnp.minimum(far, N_BUCKETS - 1))


def _toeplitz_tiles(per_dist, dist_of_m, rows, n_tiles):
    width = n_tiles * TQ
    n = width + rows + 1
    m = np.arange(n)
    m = np.where(m >= n - rows, m - n, m)
    dist = np.where(m < width, dist_of_m(m), -1)
    neg_col = per_dist.shape[1]
    idx = np.where(dist >= 0, np.minimum(dist, neg_col - 1), neg_col)
    vec = jnp.concatenate([per_dist, jnp.full((per_dist.shape[0], 1), NEG, F32)], axis=1)[:, idx]
    heads = per_dist.shape[0]
    t = jnp.tile(vec, (1, rows))[:, :rows * (n - 1)].reshape(heads, rows, n - 1)[:, :, :width]
    return t.reshape(heads, rows, n_tiles, TQ).transpose(0, 2, 1, 3)


def _bias_tables(rel_bias, seq):
    per_dist = rel_bias[_t5_bucket(jnp.arange(seq + 1))].T
    n_a = len(DIL_GROUPS) * A_HEADS
    bias_a = []
    for g, (_, dil) in enumerate(DIL_GROUPS):
        t = _toeplitz_tiles(per_dist[g * A_HEADS:(g + 1) * A_HEADS],
                            lambda m, dil=dil: np.where((m >= 0) & (m <= TQ), dil * (TQ - m), -1), TQ, 2)
        with_prev = t.transpose(0, 2, 1, 3).reshape(A_HEADS * TQ, 2 * TQ)
        no_prev = jnp.where(np.arange(2 * TQ)[None, :] < TQ, NEG, with_prev)
        bias_a.append(jnp.stack([no_prev, with_prev]))
    bias_a = jnp.stack(bias_a)
    pb = per_dist[n_a:n_a + B_HEADS]
    off_s = (N_SEL_TILES - EXT_STEP // TQ - 1) * TQ
    bias_sel = _toeplitz_tiles(pb, lambda m: off_s - m, TQ, N_SEL_TILES)
    bias_win = _toeplitz_tiles(pb, lambda m: np.where(WIN - m < WIN, WIN - m, -1), TQ, N_WIN_TILES)
    off_c = (N_C_TILES - EXT_STEP // TQ) * TQ
    bias_c = _toeplitz_tiles(per_dist[n_a + B_HEADS:], lambda m: off_c - m, TQC, N_C_TILES)
    return bias_a, bias_sel, bias_win, bias_c


def _pack_w_in(w):
    d = w.shape[0]
    a_w = 3 * len(DIL_GROUPS) * A_HEADS * HEAD_DIM
    bq_w = B_HEADS * HEAD_DIM
    bkv_w = 6 * B_KV_GROUPS * HEAD_DIM
    bg_w = 3 * B_HEADS
    c_w = 3 * C_HEADS * 2 * HEAD_DIM
    o = 0
    a = w[:, o:o + a_w]; o += a_w
    bq = w[:, o:o + bq_w]; o += bq_w
    bkv = w[:, o:o + bkv_w]; o += bkv_w
    bg = w[:, o:o + bg_w]; o += bg_w
    cc = w[:, o:o + c_w]; o += c_w
    mg = w[:, o:]
    bkv = bkv.reshape(d, 6, B_KV_GROUPS, HEAD_DIM)
    order = (2, 2, 4, 4, 3, 3, 5, 5, 0, 1)
    bkv = jnp.stack([bkv[:, o_, g] for g in range(B_KV_GROUPS) for o_ in order], axis=1)
    bkv = bkv.reshape(d, B_KV_GROUPS * BKV_UNITS * LANES)
    bg = bg.reshape(d, B_KV_GROUPS, B_HPG, 3).transpose(0, 1, 3, 2).reshape(d, B_KV_GROUPS, 3 * B_HPG)
    bg = jnp.pad(bg, ((0, 0), (0, 0), (0, LANES - 3 * B_HPG))).reshape(d, B_KV_GROUPS * LANES)
    packed = jnp.concatenate([bkv, bg, bq, a, cc, mg], axis=1)
    assert packed.shape[1] == NCOL
    return packed.astype(BF16)


def _to_residue_major(a, bsz, seq, dil):
    w = a.shape[-1]
    return a.reshape(bsz, seq // dil, dil, w).transpose(0, 2, 1, 3).reshape(bsz * seq, w)


def _from_residue_major(a, bsz, seq, dil):
    w = a.shape[-1]
    return a.reshape(bsz, dil, seq // dil, w).transpose(0, 2, 1, 3).reshape(bsz * seq, w)


def kernel(x, rel_bias, w_in, cmp_k_pos, cmp_v_pos, cmp_k_w1, cmp_k_w2, cmp_v_w1, cmp_v_w2, diff_lambda, diff_norm, w_branch_a, w_branch_b, w_branch_c, w_out, norm_mix_pre, norm_mix_post, norm_mlp_pre, norm_mlp_post, w_up, w_down):
    bsz, seq, d = x.shape
    n = bsz * seq
    depth = w_in.shape[0]
    n_grp = len(DIL_GROUPS)
    aw = A_HEADS * HEAD_DIM
    nch = seq // CMP_STRIDE
    assert seq % EXT_STEP == 0 and seq // TQ == N_SEL_TILES - EXT_STEP // TQ and (U_MGATE * LANES) % d == 0

    bias_a, bias_sel, bias_win, bias_c = _bias_tables(rel_bias, seq)
    e_mat = jnp.asarray(np.arange(seq)[None, :] // SEL_LEN == np.arange(seq // SEL_LEN)[:, None], BF16)
    starts = np.arange(nch)[:, None] * CMP_STRIDE
    bstart = np.arange(seq // SEL_LEN)[None, :] * SEL_LEN
    ov_mat = jnp.asarray((starts < bstart + SEL_LEN) & (starts + CMP_LEN > bstart), BF16)

    xf = x.reshape(n, d)
    for l in range(depth):
        lam_init = 0.8 - 0.6 * math.exp(-0.3 * l)
        p = _proj(xf, norm_mix_pre[l][None], _pack_w_in(w_in[l]))
        p3 = p.reshape(bsz, seq, NCOL)

        a = p[:, U_A * LANES:(U_A + 3 * n_grp * aw // LANES) * LANES].reshape(n, 3, n_grp, aw)
        qkv = jnp.stack([_to_residue_major(a[:, :, g].reshape(n, 3 * aw), bsz, seq, dil)
                         for g, (_, dil) in enumerate(DIL_GROUPS)])
        oa, lse = _attn_a(qkv, bias_a)
        oa = jnp.stack([_from_residue_major(oa[g], bsz, seq, dil) for g, (_, dil) in enumerate(DIL_GROUPS)])
        lse = jnp.stack([_from_residue_major(lse[g], bsz, seq, dil) for g, (_, dil) in enumerate(DIL_GROUPS)])

        kvc = jnp.stack([p3[:, :, (U_BKV + g * BKV_UNITS + BKV_UNITS - 1) * LANES:(U_BKV + (g + 1) * BKV_UNITS) * LANES]
                         for g in range(B_KV_GROUPS)], axis=1)
        xc = kvc.reshape(bsz, B_KV_GROUPS, seq, 2, HEAD_DIM).transpose(0, 1, 3, 2, 4)
        xc = xc.reshape(bsz * B_KV_GROUPS, 2, nch, CMP_STRIDE * HEAD_DIM)
        half = CMP_STRIDE * HEAD_DIM
        pos = jnp.stack([cmp_k_pos[l], cmp_v_pos[l]]).reshape(2, 2, half)
        w1 = jnp.stack([cmp_k_w1[l], cmp_v_w1[l]]).reshape(2, 2, half, CMP_HIDDEN).astype(BF16)
        w2 = jnp.stack([cmp_k_w2[l], cmp_v_w2[l]]).astype(BF16)
        kcvc = _cmp(xc, pos, w1, w2)
        ob = _nsa(p3, kcvc, bias_sel, bias_win, e_mat, ov_mat).reshape(n, B_HEADS * HEAD_DIM)

        oc = _diff(p3, bias_c, diff_lambda[l], diff_norm[l][None], lam_init).reshape(n, C_HEADS * 2 * HEAD_DIM)

        xf = _mixout(xf, oa, lse, ob, oc, p,
                     w_branch_a[l].astype(BF16), w_branch_b[l].astype(BF16), w_branch_c[l].astype(BF16),
                     w_out[l].astype(BF16), norm_mix_post[l][None])
        xf = _mlp(xf, norm_mlp_pre[l][None], w_up[l].astype(BF16), w_down[l].astype(BF16),
                  norm_mlp_post[l][None])
    return xf.reshape(bsz, seq, d)
```

```python
import functools
import math

import jax
import jax.numpy as jnp
import numpy as np
from jax import lax
from jax.experimental import pallas as pl
from jax.experimental.pallas import tpu as pltpu

F32 = jnp.float32
BF16 = jnp.bfloat16

LANES = 128
HEAD_DIM = 64
SCALE = HEAD_DIM ** -0.5
LOG2E = math.log2(math.e)
LN2 = math.log(2.0)
NORM_EPS = 1e-6
NEG = -1e30
TINY = 1e-30
TQ = 128
TQC = 256
EXT_STEP = 512

DIL_GROUPS = ((128, 1), (512, 4), (2048, 16))
A_HEADS = 8
A_PAIRS = A_HEADS // 2
B_HEADS = 16
B_KV_GROUPS = 2
B_HPG = B_HEADS // B_KV_GROUPS
C_HEADS = 8
CMP_LEN, CMP_STRIDE, CMP_HIDDEN = 32, 16, 256
SEL_LEN, SEL_TOPK, SEL_LOCAL = 64, 16, 2
FORCED_SCORE = 1e6
WIN = 512
N_BUCKETS = 32
BIAS_MAX_DIST = 2048

A_UNITS = 3 * len(DIL_GROUPS) * A_PAIRS
U_MGATE = 0
U_BQ = 48
U_C = 56
U_BKV = 80
BKV_UNITS = 5
U_BGATE = 90
U_TOTAL = 92
NCOL = U_TOTAL * LANES

VMEM_LIMIT = 56 * 1024 * 1024


def _cparams(sem):
    return pltpu.CompilerParams(dimension_semantics=sem, vmem_limit_bytes=VMEM_LIMIT)


def _rms(x, gain):
    return x * lax.rsqrt(jnp.mean(x * x, axis=-1, keepdims=True) + NORM_EPS) * gain


def _dot_nt(a, b):
    return lax.dot_general(a, b, (((1,), (1,)), ((), ())), preferred_element_type=F32)


def _dot(a, b):
    return jnp.dot(a, b, preferred_element_type=F32)


def _lane_half(shape):
    return lax.broadcasted_iota(jnp.int32, shape, len(shape) - 1) < HEAD_DIM


def _stack_halves(tiles):
    low = _lane_half(tiles[0].shape)
    out = []
    for t in tiles:
        z = jnp.zeros_like(t)
        out += [jnp.where(low, t, z), jnp.where(low, z, t)]
    return jnp.concatenate(out, axis=0)


def _softmax_pv(s, v):
    m = jnp.max(s, axis=-1, keepdims=True)
    p = jnp.exp2(s - m)
    l = jnp.sum(p, axis=-1, keepdims=True)
    return _dot(p.astype(BF16), v) / l


def _proj_kernel(x_ref, g_ref, w_ref, oa_ref, o_ref, h_ref, *, n_a):
    j = pl.program_id(1)

    @pl.when(j == 0)
    def _():
        h_ref[...] = _rms(x_ref[...], g_ref[...]).astype(BF16)

    res = _dot(h_ref[...], w_ref[...])

    @pl.when(j < n_a)
    def _():
        for c in range(oa_ref.shape[0]):
            oa_ref[c] = res[:, c * LANES:(c + 1) * LANES]

    @pl.when(j >= n_a)
    def _():
        o_ref[...] = res.astype(o_ref.dtype)


def _proj(x, gain, w, tm=1024, tn=512):
    n, d = x.shape
    n_a = A_UNITS * LANES // tn
    upb = tn // LANES
    return pl.pallas_call(
        functools.partial(_proj_kernel, n_a=n_a),
        out_shape=(jax.ShapeDtypeStruct((A_UNITS, n, LANES), F32), jax.ShapeDtypeStruct((n, NCOL), BF16)),
        grid=(n // tm, w.shape[1] // tn),
        in_specs=[pl.BlockSpec((tm, d), lambda i, j: (i, 0)),
                  pl.BlockSpec((1, d), lambda i, j: (0, 0)),
                  pl.BlockSpec((d, tn), lambda i, j: (0, j))],
        out_specs=(pl.BlockSpec((upb, tm, LANES), lambda i, j: (jnp.minimum(j, n_a - 1), i, 0)),
                   pl.BlockSpec((tm, tn), lambda i, j: (i, jnp.maximum(j - n_a, 0)))),
        scratch_shapes=[pltpu.VMEM((tm, d), BF16)],
        compiler_params=_cparams(("parallel", "arbitrary")),
        name="proj",
    )(x, gain, w)


def _attn_a_kernel(*refs, dil, use_prev):
    if use_prev:
        q_ref, kc_ref, vc_ref, kp_ref, vp_ref, bias_ref, o_ref, lse_ref = refs
    else:
        q_ref, kc_ref, vc_ref, bias_ref, o_ref, lse_ref = refs
    sb = pl.program_id(1)
    r = pl.program_id(2)
    rows = pl.ds(r, TQ, stride=dil) if dil > 1 else slice(None)
    ld = lambda ref, p: ref[p, rows, :].astype(BF16)
    s = []
    for p in range(A_PAIRS):
        qs = _stack_halves([ld(q_ref, p)])
        ks = jnp.concatenate([ld(kp_ref, p), ld(kc_ref, p)], axis=0) if use_prev else ld(kc_ref, p)
        s.append(_dot_nt(qs, ks))
    bias = bias_ref[(sb > 0).astype(jnp.int32)] if use_prev else bias_ref[...]
    s = jnp.concatenate(s, axis=0) + bias
    m = jnp.max(s, axis=-1, keepdims=True)
    e = jnp.exp2(s - m)
    l = jnp.sum(e, axis=-1, keepdims=True)
    pb = e.astype(BF16)
    lse = (m + jnp.log2(l)) * LN2
    low = _lane_half((TQ, LANES))
    for p in range(A_PAIRS):
        r0 = 2 * p * TQ
        vs = jnp.concatenate([ld(vp_ref, p), ld(vc_ref, p)], axis=0) if use_prev else ld(vc_ref, p)
        o = _dot(pb[r0:r0 + 2 * TQ], vs) / l[r0:r0 + 2 * TQ]
        o_ref[p, rows, :] = jnp.where(low, o[:TQ], o[TQ:])
        lse_ref[p, rows, :] = jnp.where(low, lse[r0:r0 + TQ], lse[r0 + TQ:r0 + 2 * TQ])


def _attn_a(pa, bias, g, dil, bsz, seq):
    n = pa.shape[1]
    blk = TQ * dil
    nsb = seq // blk
    use_prev = nsb > 1
    n_grp = len(DIL_GROUPS)
    cur = lambda qkv: pl.BlockSpec((A_PAIRS, blk, LANES), lambda b, sb, r: (qkv * n_grp + g, b * nsb + sb, 0))
    prev = lambda qkv: pl.BlockSpec((A_PAIRS, blk, LANES),
                                    lambda b, sb, r: (qkv * n_grp + g, b * nsb + jnp.maximum(sb - 1, 0), 0))
    out = pl.BlockSpec((A_PAIRS, blk, LANES), lambda b, sb, r: (0, b * nsb + sb, 0))
    in_specs = [cur(0), cur(1), cur(2)] + ([prev(1), prev(2)] if use_prev else [])
    in_specs.append(pl.BlockSpec(bias.shape, lambda b, sb, r: (0,) * bias.ndim))
    args = [pa] * (len(in_specs) - 1) + [bias]
    return pl.pallas_call(
        functools.partial(_attn_a_kernel, dil=dil, use_prev=use_prev),
        out_shape=(jax.ShapeDtypeStruct((A_PAIRS, n, LANES), F32),) * 2,
        grid=(bsz, nsb, dil),
        in_specs=in_specs,
        out_specs=(out, out),
        compiler_params=_cparams(("parallel", "parallel", "arbitrary")),
        name=f"attn_a_d{dil}",
    )(*args)


def _cmp_kernel(x_ref, pos_ref, w1_ref, w2_ref, o_ref):
    for kv in range(2):
        x = x_ref[kv].astype(F32)
        top = (x + pos_ref[kv, 0:1, :]).astype(BF16)
        bot = (x + pos_ref[kv, 1:2, :]).astype(BF16)
        p1 = _dot(top, w1_ref[kv, 0])
        p2 = _dot(bot, w1_ref[kv, 1])
        hid = p1 + pltpu.roll(p2, p2.shape[0] - 1, 0)
        out = _dot(jax.nn.gelu(hid).astype(BF16), w2_ref[kv]).astype(o_ref.dtype)
        o_ref[kv] = jnp.concatenate([out, out], axis=1)


def _cmp(xc, pos, w1, w2):
    nb = xc.shape[0]
    nch, cw = xc.shape[2], xc.shape[3]
    full = lambda a: pl.BlockSpec(a.shape, lambda i: (0,) * a.ndim)
    return pl.pallas_call(
        _cmp_kernel,
        out_shape=jax.ShapeDtypeStruct((nb, 2, nch, LANES), BF16),
        grid=(nb,),
        in_specs=[pl.BlockSpec((None, 2, nch, cw), lambda i: (i, 0, 0, 0)),
                  full(pos), full(w1), full(w2)],
        out_specs=pl.BlockSpec((None, 2, nch, LANES), lambda i: (i, 0, 0, 0)),
        compiler_params=_cparams(("parallel",)),
        name="cmp",
    )(xc, pos, w1, w2)


N_SEL_TILES = 20
N_WIN_TILES = 9


def _nsa_kernel(q_ref, kv_ref, gate_ref, kcvc_ref, bsel_ref, bwin_ref, et_ref, ov_ref, o_ref):
    i = pl.program_id(2)
    seq = kv_ref.shape[0]
    rows = lax.broadcasted_iota(jnp.int32, (TQ, 1), 0)
    t = i * TQ + rows
    q = q_ref[...]
    qs = _stack_halves([q[:, p * LANES:(p + 1) * LANES] for p in range(B_HPG // 2)])
    gates = jax.nn.sigmoid(gate_ref[...].astype(F32))
    head = lambda a, h: a[h * TQ:(h + 1) * TQ]

    kc = kcvc_ref[0]
    vc = kcvc_ref[1]
    n_cmp = kc.shape[0]
    n_idx = lax.broadcasted_iota(jnp.int32, (1, n_cmp), 1)
    cmask = (n_idx * CMP_STRIDE + (CMP_LEN - 1)) <= t
    c_add = jnp.concatenate([jnp.where(cmask, 0.0, NEG)] * B_HPG, axis=0)
    c_mul = jnp.concatenate([cmask.astype(F32)] * B_HPG, axis=0)
    lc = _dot_nt(qs, kc) + c_add
    e = jnp.exp2(lc - jnp.max(lc, axis=-1, keepdims=True)) * c_mul
    p_cmp = e / jnp.maximum(jnp.sum(e, axis=-1, keepdims=True), TINY)
    o_cmp = _dot(p_cmp.astype(BF16), vc)
    psum = head(p_cmp, 0)
    for h in range(1, B_HPG):
        psum = psum + head(p_cmp, h)

    hi = psum.astype(BF16)
    lo = (psum - hi.astype(F32)).astype(BF16)
    score = _dot(hi, ov_ref[...]) + _dot(lo, ov_ref[...])
    n_sel = seq // SEL_LEN
    jb = lax.broadcasted_iota(jnp.int32, (1, LANES), 1)
    back = lax.shift_right_arithmetic(t, int(math.log2(SEL_LEN))) - jb
    forced = (jb == 0) | ((back >= 0) & (back < SEL_LOCAL))
    score = jnp.where(forced, FORCED_SCORE, score)
    score = jnp.where(jb * SEL_LEN <= t, score, NEG)
    rank = jnp.zeros((TQ, LANES), jnp.int32)
    for ii in range(n_sel):
        col = score[:, ii:ii + 1]
        beats = (col > score) | ((col == score) & (ii < jb))
        rank = rank + beats.astype(jnp.int32)
    sel_neg = jnp.where(rank < SEL_TOPK, 0.0, NEG).astype(BF16)
    qs_sel = jnp.concatenate([qs, jnp.concatenate([sel_neg] * B_HPG, axis=0)], axis=1)

    n_wt = WIN // TQ + 1
    kt0 = jnp.maximum(i - WIN // TQ, 0)
    k0 = pl.multiple_of(kt0 * TQ, TQ)
    s_w = _dot_nt(qs, kv_ref[pl.ds(k0, n_wt * TQ), LANES:2 * LANES])
    w0 = WIN // TQ - i + kt0
    b_w = jnp.concatenate(
        [jnp.concatenate([bwin_ref[h, w0 + kk] for kk in range(n_wt)], axis=1) for h in range(B_HPG)], axis=0)
    o_win = _softmax_pv(s_w + b_w, kv_ref[pl.ds(k0, n_wt * TQ), 3 * LANES:4 * LANES])

    def finish(o_sel):
        low = _lane_half((TQ, LANES))
        outs = []
        for h in range(B_HPG):
            outs.append(gates[:, h:h + 1] * head(o_cmp, h)
                        + gates[:, B_HPG + h:B_HPG + h + 1] * head(o_sel, h)
                        + gates[:, 2 * B_HPG + h:2 * B_HPG + h + 1] * head(o_win, h))
        for p in range(B_HPG // 2):
            o_ref[:, p * LANES:(p + 1) * LANES] = jnp.where(low, outs[2 * p], outs[2 * p + 1]).astype(o_ref.dtype)

    tiles_per_ext = EXT_STEP // TQ
    for v in range(seq // EXT_STEP):
        @pl.when(i // tiles_per_ext == v)
        def _(v=v):
            ext = (v + 1) * EXT_STEP
            ks = jnp.concatenate([kv_ref[0:ext, 0:LANES], et_ref[0:ext, :]], axis=1)
            b0 = (N_SEL_TILES - tiles_per_ext - 1) - i
            b_s = jnp.concatenate(
                [jnp.concatenate([bsel_ref[h, b0 + kk] for kk in range(ext // TQ)], axis=1)
                 for h in range(B_HPG)], axis=0)
            finish(_softmax_pv(_dot_nt(qs_sel, ks) + b_s, kv_ref[0:ext, 2 * LANES:3 * LANES]))


def _nsa(p3, kcvc, bsel, bwin, et_mat, ov_mat):
    b, s, _ = p3.shape
    nq = s // TQ
    qw = B_HPG * HEAD_DIM
    kvw = BKV_UNITS * LANES
    full = lambda a: pl.BlockSpec(a.shape, lambda g, bb, i: (0,) * a.ndim)
    per_group = lambda a: pl.BlockSpec((B_HPG,) + a.shape[1:], lambda g, bb, i: (g, 0, 0, 0),
                                       pipeline_mode=pl.Buffered(1))
    return pl.pallas_call(
        _nsa_kernel,
        out_shape=jax.ShapeDtypeStruct((b, s, B_HEADS * HEAD_DIM), BF16),
        grid=(B_KV_GROUPS, b, nq),
        in_specs=[
            pl.BlockSpec((None, TQ, qw), lambda g, bb, i: (bb, i, U_BQ * LANES // qw + g)),
            pl.BlockSpec((None, s, kvw), lambda g, bb, i: (bb, 0, U_BKV * LANES // kvw + g)),
            pl.BlockSpec((None, TQ, LANES), lambda g, bb, i: (bb, i, U_BGATE + g)),
            pl.BlockSpec((None, 2, kcvc.shape[2], LANES), lambda g, bb, i: (bb * B_KV_GROUPS + g, 0, 0, 0)),
            per_group(bsel), per_group(bwin), full(et_mat), full(ov_mat)],
        out_specs=pl.BlockSpec((None, TQ, qw), lambda g, bb, i: (bb, i, g)),
        compiler_params=_cparams(("parallel", "parallel", "arbitrary")),
        name="nsa",
    )(p3, p3, p3, kcvc, bsel, bwin, et_mat, ov_mat)


N_C_TILES = 18


def _diff_kernel(q_ref, k_ref, v_ref, bias_ref, lam_ref, gain_ref, o_ref, *, lam_init):
    i = pl.program_id(2)
    seq = k_ref.shape[0]
    lv = lam_ref[...]
    lam = (jnp.exp(jnp.sum(lv[0:1] * lv[1:2], axis=-1, keepdims=True))
           - jnp.exp(jnp.sum(lv[2:3] * lv[3:4], axis=-1, keepdims=True)) + lam_init)
    qs = _stack_halves([q_ref[...]])
    q_per_ext = EXT_STEP // TQC
    for v in range(seq // EXT_STEP):
        @pl.when(i // q_per_ext == v)
        def _(v=v):
            ext = (v + 1) * EXT_STEP
            b0 = (N_C_TILES - EXT_STEP // TQ) - 2 * i
            bias = jnp.concatenate([bias_ref[b0 + kk] for kk in range(ext // TQ)], axis=1)
            s = _dot_nt(qs, k_ref[0:ext, :]) + jnp.concatenate([bias, bias], axis=0)
            o = _softmax_pv(s, v_ref[0:ext, :])
            o = o[:TQC] - lam * o[TQC:]
            o_ref[...] = (_rms(o, gain_ref[...]) * (1.0 - lam_init)).astype(o_ref.dtype)


def _diff(p3, bias, lam_vecs, sub_gain, lam_init):
    b, s, _ = p3.shape
    dv = 2 * HEAD_DIM
    full = lambda a: pl.BlockSpec(a.shape, lambda h, bb, i: (0,) * a.ndim)
    return pl.pallas_call(
        functools.partial(_diff_kernel, lam_init=lam_init),
        out_shape=jax.ShapeDtypeStruct((b, s, C_HEADS * dv), BF16),
        grid=(C_HEADS, b, s // TQC),
        in_specs=[
            pl.BlockSpec((None, TQC, dv), lambda h, bb, i: (bb, i, U_C + h)),
            pl.BlockSpec((None, s, dv), lambda h, bb, i: (bb, 0, U_C + C_HEADS + h)),
            pl.BlockSpec((None, s, dv), lambda h, bb, i: (bb, 0, U_C + 2 * C_HEADS + h)),
            pl.BlockSpec((None,) + bias.shape[1:], lambda h, bb, i: (h, 0, 0, 0)),
            full(lam_vecs), full(sub_gain)],
        out_specs=pl.BlockSpec((None, TQC, dv), lambda h, bb, i: (bb, i, h)),
        compiler_params=_cparams(("parallel", "parallel", "arbitrary")),
        name="diff",
    )(p3, p3, p3, bias, lam_vecs, sub_gain)


def _mixout_kernel(x_ref, oa0_ref, oa1_ref, oa2_ref, l0_ref, l1_ref, l2_ref, ob_ref, oc_ref,
                   g0_ref, g1_ref, g2_ref, wa_ref, wb_ref, wc_ref, wo_ref, gain_ref, o_ref):
    planes = lambda ref: jnp.concatenate([ref[p] for p in range(ref.shape[0])], axis=1)
    lse = [planes(l0_ref), planes(l1_ref), planes(l2_ref)]
    m = jnp.maximum(jnp.maximum(lse[0], lse[1]), lse[2])
    e = [jnp.exp(v - m) for v in lse]
    den = e[0] + e[1] + e[2]
    oa = (e[0] / den) * planes(oa0_ref) + (e[1] / den) * planes(oa1_ref) + (e[2] / den) * planes(oa2_ref)
    mixed = (jax.nn.sigmoid(g0_ref[...].astype(F32)) * _dot(oa.astype(BF16), wa_ref[...])
             + jax.nn.sigmoid(g1_ref[...].astype(F32)) * _dot(ob_ref[...], wb_ref[...])
             + jax.nn.sigmoid(g2_ref[...].astype(F32)) * _dot(oc_ref[...], wc_ref[...]))
    y = _dot(mixed.astype(BF16), wo_ref[...])
    o_ref[...] = x_ref[...] + _rms(y, gain_ref[...])


def _mixout(x, oa, lse, ob, oc, p, wa, wb, wc, wo, gain, tm=256):
    n, d = x.shape
    const = lambda a: pl.BlockSpec(a.shape, lambda i: (0,) * a.ndim, pipeline_mode=pl.Buffered(1))
    row = lambda a: pl.BlockSpec((tm, a.shape[1]), lambda i: (i, 0))
    grp = lambda a: pl.BlockSpec((a.shape[0], tm, a.shape[2]), lambda i: (0, i, 0))
    gate = lambda br: pl.BlockSpec((tm, d), lambda i: (i, U_MGATE * LANES // d + br))
    return pl.pallas_call(
        _mixout_kernel,
        out_shape=jax.ShapeDtypeStruct((n, d), F32),
        grid=(n // tm,),
        in_specs=[row(x)] + [grp(a) for a in oa] + [grp(a) for a in lse] + [row(ob), row(oc),
                  gate(0), gate(1), gate(2), const(wa), const(wb), const(wc), const(wo), const(gain)],
        out_specs=row(x),
        compiler_params=_cparams(("parallel",)),
        name="mixout",
    )(x, *oa, *lse, ob, oc, p, p, p, wa, wb, wc, wo, gain)


def _mlp_kernel(x_ref, gpre_ref, wup_ref, wdn_ref, gpost_ref, o_ref, h_ref, acc_ref):
    f = pl.program_id(1)

    @pl.when(f == 0)
    def _():
        h_ref[...] = _rms(x_ref[...], gpre_ref[...]).astype(BF16)
        acc_ref[...] = jnp.zeros_like(acc_ref)

    u = jnp.square(jax.nn.relu(_dot(h_ref[...], wup_ref[...])))
    acc_ref[...] += _dot(u.astype(BF16), wdn_ref[...])

    @pl.when(f == pl.num_programs(1) - 1)
    def _():
        o_ref[...] = x_ref[...] + _rms(acc_ref[...], gpost_ref[...])


def _mlp(x, gpre, wup, wdn, gpost, tm=512, tf=1024):
    n, d = x.shape
    dff = wup.shape[1]
    return pl.pallas_call(
        _mlp_kernel,
        out_shape=jax.ShapeDtypeStruct((n, d), F32),
        grid=(n // tm, dff // tf),
        in_specs=[pl.BlockSpec((tm, d), lambda i, f: (i, 0)),
                  pl.BlockSpec((1, d), lambda i, f: (0, 0)),
                  pl.BlockSpec((d, tf), lambda i, f: (0, f)),
                  pl.BlockSpec((tf, d), lambda i, f: (f, 0)),
                  pl.BlockSpec((1, d), lambda i, f: (0, 0))],
        out_specs=pl.BlockSpec((tm, d), lambda i, f: (i, 0)),
        scratch_shapes=[pltpu.VMEM((tm, d), BF16), pltpu.VMEM((tm, d), F32)],
        compiler_params=_cparams(("parallel", "arbitrary")),
        name="mlp",
    )(x, gpre, wup, wdn, gpost)


def _toeplitz_kernel(v_ref, o_ref):
    n_tiles, rows, _ = o_ref.shape
    x = jnp.broadcast_to(v_ref[...], (rows, v_ref.shape[-1]))
    t = pltpu.roll(x, 0, 1, stride=1, stride_axis=0)
    for k in range(n_tiles):
        o_ref[k] = t[:, k * TQ:(k + 1) * TQ]


def _toeplitz_tiles(per_dist, dist_of_m, rows, n_tiles):
    width = n_tiles * TQ
    w = width + rows
    m = np.arange(w)
    m = np.where(m >= width, m - w, m)
    dist = dist_of_m(m)
    neg_col = per_dist.shape[1]
    idx = np.where(dist >= 0, np.minimum(dist, neg_col - 1), neg_col)
    heads = per_dist.shape[0]
    vec = jnp.concatenate([per_dist, jnp.full((heads, 1), NEG, F32)], axis=1)[:, idx]
    return pl.pallas_call(
        _toeplitz_kernel,
        out_shape=jax.ShapeDtypeStruct((heads, n_tiles, rows, TQ), F32),
        grid=(heads,),
        in_specs=[pl.BlockSpec((None, 1, w), lambda h: (h, 0, 0))],
        out_specs=pl.BlockSpec((None, n_tiles, rows, TQ), lambda h: (h, 0, 0, 0)),
        compiler_params=_cparams(("parallel",)),
        name="toeplitz",
    )(vec.reshape(heads, 1, w))


def _t5_bucket(dist):
    exact = N_BUCKETS // 2
    d = jnp.maximum(dist, 0)
    logd = jnp.log(jnp.maximum(d, 1).astype(F32) / exact)
    far = exact + (logd / math.log(BIAS_MAX_DIST / exact) * (N_BUCKETS - exact)).astype(jnp.int32)
    return jnp.where(d < exact, d, jnp.minimum(far, N_BUCKETS - 1))


def _bias_tables(rel_bias, seq):
    per_dist = rel_bias[_t5_bucket(jnp.arange(seq + 1))].T * LOG2E
    n_a = len(DIL_GROUPS) * A_HEADS
    bias_a = []
    for g, (_, dil) in enumerate(DIL_GROUPS):
        t = _toeplitz_tiles(per_dist[g * A_HEADS:(g + 1) * A_HEADS],
                            lambda m, dil=dil: np.where((m >= 0) & (m <= TQ), dil * (TQ - m), -1), TQ, 2)
        if seq // (TQ * dil) > 1:
            with_prev = t.transpose(0, 2, 1, 3).reshape(A_HEADS * TQ, 2 * TQ)
            no_prev = jnp.where(np.arange(2 * TQ)[None, :] < TQ, NEG, with_prev)
            bias_a.append(jnp.stack([no_prev, with_prev]))
        else:
            bias_a.append(t[:, 1].reshape(A_HEADS * TQ, TQ))
    pb = per_dist[n_a:n_a + B_HEADS]
    off_s = (N_SEL_TILES - EXT_STEP // TQ - 1) * TQ
    bias_sel = _toeplitz_tiles(pb, lambda m: off_s - m, TQ, N_SEL_TILES)
    bias_win = _toeplitz_tiles(pb, lambda m: np.where(WIN - m < WIN, WIN - m, -1), TQ, N_WIN_TILES)
    off_c = (N_C_TILES - EXT_STEP // TQ) * TQ
    bias_c = _toeplitz_tiles(per_dist[n_a + B_HEADS:], lambda m: off_c - m, TQC, N_C_TILES)
    return bias_a, bias_sel, bias_win, bias_c


def _pack_w_in(w):
    d = w.shape[0]
    a_w = A_UNITS * LANES
    aq_w = a_w // 3
    bq_w = B_HEADS * HEAD_DIM
    bkv_w = 6 * B_KV_GROUPS * HEAD_DIM
    bg_w = 3 * B_HEADS
    c_w = 3 * C_HEADS * 2 * HEAD_DIM
    cq_w = c_w // 3
    o_bq, o_bkv, o_bg, o_c, o_mg = a_w, a_w + bq_w, a_w + bq_w + bkv_w, a_w + bq_w + bkv_w + bg_w, \
        a_w + bq_w + bkv_w + bg_w + c_w
    colscale = np.ones((w.shape[1],), np.float32)
    for lo, n_q in ((0, aq_w), (o_bq, bq_w), (o_c, cq_w)):
        colscale[lo:lo + n_q] = SCALE * LOG2E
    wb = (w * colscale).astype(BF16)
    a, bq, bkv, bg, cc, mg = (wb[:, :a_w], wb[:, o_bq:o_bkv], wb[:, o_bkv:o_bg], wb[:, o_bg:o_c],
                              wb[:, o_c:o_mg], wb[:, o_mg:])
    bkv = bkv.reshape(d, 6, B_KV_GROUPS, HEAD_DIM)
    order = (2, 2, 4, 4, 3, 3, 5, 5, 0, 1)
    bkv = jnp.stack([bkv[:, o_, g] for g in range(B_KV_GROUPS) for o_ in order], axis=1)
    bkv = bkv.reshape(d, B_KV_GROUPS * BKV_UNITS * LANES)
    bg = bg.reshape(d, B_KV_GROUPS, B_HPG, 3).transpose(0, 1, 3, 2).reshape(d, B_KV_GROUPS, 3 * B_HPG)
    bg = jnp.pad(bg, ((0, 0), (0, 0), (0, LANES - 3 * B_HPG))).reshape(d, B_KV_GROUPS * LANES)
    packed = jnp.concatenate([a, mg, bq, cc, bkv, bg], axis=1)
    assert packed.shape[1] == a_w + NCOL
    return packed


def kernel(x, rel_bias, w_in, cmp_k_pos, cmp_v_pos, cmp_k_w1, cmp_k_w2, cmp_v_w1, cmp_v_w2, diff_lambda, diff_norm, w_branch_a, w_branch_b, w_branch_c, w_out, norm_mix_pre, norm_mix_post, norm_mlp_pre, norm_mlp_post, w_up, w_down):
    bsz, seq, d = x.shape
    n = bsz * seq
    depth = w_in.shape[0]
    nch = seq // CMP_STRIDE
    assert seq % EXT_STEP == 0 and seq // TQ == N_SEL_TILES - EXT_STEP // TQ and (U_MGATE * LANES) % d == 0

    bias_a, bias_sel, bias_win, bias_c = _bias_tables(rel_bias, seq)
    et_mat = jnp.asarray(np.arange(seq)[:, None] // SEL_LEN == np.arange(LANES)[None, :], BF16)
    starts = np.arange(nch)[:, None] * CMP_STRIDE
    bstart = np.arange(LANES)[None, :] * SEL_LEN
    ov_mat = jnp.asarray((starts < bstart + SEL_LEN) & (starts + CMP_LEN > bstart), BF16)

    xf = x.reshape(n, d)
    for l in range(depth):
        lam_init = 0.8 - 0.6 * math.exp(-0.3 * l)
        pa, p = _proj(xf, norm_mix_pre[l][None], _pack_w_in(w_in[l]))
        p3 = p.reshape(bsz, seq, NCOL)

        a_out = [_attn_a(pa, bias_a[g], g, dil, bsz, seq) for g, (_, dil) in enumerate(DIL_GROUPS)]
        oa = [o for o, _ in a_out]
        lse = [s for _, s in a_out]

        kvc = jnp.stack([p3[:, :, (U_BKV + g * BKV_UNITS + BKV_UNITS - 1) * LANES:(U_BKV + (g + 1) * BKV_UNITS) * LANES]
                         for g in range(B_KV_GROUPS)], axis=1)
        xc = kvc.reshape(bsz, B_KV_GROUPS, seq, 2, HEAD_DIM).transpose(0, 1, 3, 2, 4)
        xc = xc.reshape(bsz * B_KV_GROUPS, 2, nch, CMP_STRIDE * HEAD_DIM)
        half = CMP_STRIDE * HEAD_DIM
        pos = jnp.stack([cmp_k_pos[l], cmp_v_pos[l]]).reshape(2, 2, half)
        w1 = jnp.stack([cmp_k_w1[l], cmp_v_w1[l]]).reshape(2, 2, half, CMP_HIDDEN).astype(BF16)
        w2 = jnp.stack([cmp_k_w2[l], cmp_v_w2[l]]).astype(BF16)
        kcvc = _cmp(xc, pos, w1, w2)
        ob = _nsa(p3, kcvc, bias_sel, bias_win, et_mat, ov_mat).reshape(n, B_HEADS * HEAD_DIM)

        oc = _diff(p3, bias_c, diff_lambda[l], diff_norm[l][None], lam_init).reshape(n, C_HEADS * 2 * HEAD_DIM)

        xf = _mixout(xf, oa, lse, ob, oc, p,
                     w_branch_a[l].astype(BF16), w_branch_b[l].astype(BF16), w_branch_c[l].astype(BF16),
                     w_out[l].astype(BF16), norm_mix_post[l][None])
        xf = _mlp(xf, norm_mlp_pre[l][None], w_up[l].astype(BF16), w_down[l].astype(BF16),
                  norm_mlp_post[l][None])
    return xf.reshape(bsz, seq, d)
```

```python
import functools
import math

import jax
import jax.numpy as jnp
import numpy as np
from jax import lax
from jax.experimental import pallas as pl
from jax.experimental.pallas import tpu as pltpu

F32 = jnp.float32
BF16 = jnp.bfloat16

LANES = 128
HEAD_DIM = 64
SCALE = HEAD_DIM ** -0.5
LOG2E = math.log2(math.e)
LN2 = math.log(2.0)
NORM_EPS = 1e-6
NEG = -1e30
TINY = 1e-30
TQ = 128
TQC = 256
EXT_STEP = 512

DIL_GROUPS = ((128, 1), (512, 4), (2048, 16))
A_HEADS = 8
A_PAIRS = A_HEADS // 2
B_HEADS = 16
B_KV_GROUPS = 2
B_HPG = B_HEADS // B_KV_GROUPS
C_HEADS = 8
CMP_LEN, CMP_STRIDE, CMP_HIDDEN = 32, 16, 256
SEL_LEN, SEL_TOPK, SEL_LOCAL = 64, 16, 2
FORCED_SCORE = 1e6
WIN = 512
N_BUCKETS = 32
BIAS_MAX_DIST = 2048

A_UNITS = 3 * len(DIL_GROUPS) * A_PAIRS
U_MGATE = 0
U_BQ = 48
U_C = 56
U_BKV = 80
BKV_UNITS = 5
U_BGATE = 90
U_TOTAL = 92
NCOL = U_TOTAL * LANES

VMEM_LIMIT = 56 * 1024 * 1024


def _cparams(sem):
    return pltpu.CompilerParams(dimension_semantics=sem, vmem_limit_bytes=VMEM_LIMIT)


def _rms(x, gain):
    return x * lax.rsqrt(jnp.mean(x * x, axis=-1, keepdims=True) + NORM_EPS) * gain


def _dot_nt(a, b):
    return lax.dot_general(a, b, (((1,), (1,)), ((), ())), preferred_element_type=F32)


def _dot(a, b):
    return jnp.dot(a, b, preferred_element_type=F32)


def _lane_half(shape):
    return lax.broadcasted_iota(jnp.int32, shape, len(shape) - 1) < HEAD_DIM


def _stack_halves(tiles):
    low = _lane_half(tiles[0].shape)
    out = []
    for t in tiles:
        z = jnp.zeros_like(t)
        out += [jnp.where(low, t, z), jnp.where(low, z, t)]
    return jnp.concatenate(out, axis=0)


def _softmax_pv(s, v):
    m = jnp.max(s, axis=-1, keepdims=True)
    p = jnp.exp2(s - m)
    l = jnp.sum(p, axis=-1, keepdims=True)
    return _dot(p.astype(BF16), v) / l


def _proj_kernel(x_ref, g_ref, w_ref, oa_ref, o_ref, h_ref, *, n_a):
    j = pl.program_id(1)

    @pl.when(j == 0)
    def _():
        h_ref[...] = _rms(x_ref[...], g_ref[...]).astype(BF16)

    res = _dot(h_ref[...], w_ref[...])

    @pl.when(j < n_a)
    def _():
        for c in range(oa_ref.shape[0]):
            oa_ref[c] = res[:, c * LANES:(c + 1) * LANES]

    @pl.when(j >= n_a)
    def _():
        o_ref[...] = res.astype(o_ref.dtype)


def _proj(x, gain, w, tm=1024, tn=512):
    n, d = x.shape
    n_a = A_UNITS * LANES // tn
    upb = tn // LANES
    return pl.pallas_call(
        functools.partial(_proj_kernel, n_a=n_a),
        out_shape=(jax.ShapeDtypeStruct((A_UNITS, n, LANES), F32), jax.ShapeDtypeStruct((n, NCOL), BF16)),
        grid=(n // tm, w.shape[1] // tn),
        in_specs=[pl.BlockSpec((tm, d), lambda i, j: (i, 0)),
                  pl.BlockSpec((1, d), lambda i, j: (0, 0)),
                  pl.BlockSpec((d, tn), lambda i, j: (0, j))],
        out_specs=(pl.BlockSpec((upb, tm, LANES), lambda i, j: (jnp.minimum(j, n_a - 1), i, 0)),
                   pl.BlockSpec((tm, tn), lambda i, j: (i, jnp.maximum(j - n_a, 0)))),
        scratch_shapes=[pltpu.VMEM((tm, d), BF16)],
        compiler_params=_cparams(("parallel", "arbitrary")),
        name="proj",
    )(x, gain, w)


def _attn_a_kernel(*refs, dil, use_prev):
    if use_prev:
        q_ref, kc_ref, vc_ref, kp_ref, vp_ref, bias_ref, o_ref, lse_ref = refs
    else:
        q_ref, kc_ref, vc_ref, bias_ref, o_ref, lse_ref = refs
    sb = pl.program_id(1)
    r = pl.program_id(2)
    rows = pl.ds(r, TQ, stride=dil) if dil > 1 else slice(None)
    ld = lambda ref, p: ref[p, rows, :].astype(BF16)
    s = []
    for p in range(A_PAIRS):
        qs = _stack_halves([ld(q_ref, p)])
        ks = jnp.concatenate([ld(kp_ref, p), ld(kc_ref, p)], axis=0) if use_prev else ld(kc_ref, p)
        s.append(_dot_nt(qs, ks))
    bias = bias_ref[(sb > 0).astype(jnp.int32)] if use_prev else bias_ref[...]
    s = jnp.concatenate(s, axis=0) + bias
    m = jnp.max(s, axis=-1, keepdims=True)
    e = jnp.exp2(s - m)
    l = jnp.sum(e, axis=-1, keepdims=True)
    pb = e.astype(BF16)
    lse = (m + jnp.log2(l)) * LN2
    low = _lane_half((TQ, LANES))
    for p in range(A_PAIRS):
        r0 = 2 * p * TQ
        vs = jnp.concatenate([ld(vp_ref, p), ld(vc_ref, p)], axis=0) if use_prev else ld(vc_ref, p)
        o = _dot(pb[r0:r0 + 2 * TQ], vs) / l[r0:r0 + 2 * TQ]
        o_ref[p, rows, :] = jnp.where(low, o[:TQ], o[TQ:])
        lse_ref[p, rows, :] = jnp.where(low, lse[r0:r0 + TQ], lse[r0 + TQ:r0 + 2 * TQ])


def _attn_a(pa, bias, g, dil, bsz, seq):
    n = pa.shape[1]
    blk = TQ * dil
    nsb = seq // blk
    use_prev = nsb > 1
    n_grp = len(DIL_GROUPS)
    cur = lambda qkv: pl.BlockSpec((A_PAIRS, blk, LANES), lambda b, sb, r: (qkv * n_grp + g, b * nsb + sb, 0))
    prev = lambda qkv: pl.BlockSpec((A_PAIRS, blk, LANES),
                                    lambda b, sb, r: (qkv * n_grp + g, b * nsb + jnp.maximum(sb - 1, 0), 0))
    out = pl.BlockSpec((A_PAIRS, blk, LANES), lambda b, sb, r: (0, b * nsb + sb, 0))
    in_specs = [cur(0), cur(1), cur(2)] + ([prev(1), prev(2)] if use_prev else [])
    in_specs.append(pl.BlockSpec(bias.shape, lambda b, sb, r: (0,) * bias.ndim))
    args = [pa] * (len(in_specs) - 1) + [bias]
    return pl.pallas_call(
        functools.partial(_attn_a_kernel, dil=dil, use_prev=use_prev),
        out_shape=(jax.ShapeDtypeStruct((A_PAIRS, n, LANES), F32),) * 2,
        grid=(bsz, nsb, dil),
        in_specs=in_specs,
        out_specs=(out, out),
        compiler_params=_cparams(("parallel", "parallel", "arbitrary")),
        name=f"attn_a_d{dil}",
    )(*args)


def _cmp_kernel(x_ref, pos_ref, w1_ref, w2_ref, o_ref):
    for kv in range(2):
        x = x_ref[kv].astype(F32)
        top = (x + pos_ref[kv, 0:1, :]).astype(BF16)
        bot = (x + pos_ref[kv, 1:2, :]).astype(BF16)
        p1 = _dot(top, w1_ref[kv, 0])
        p2 = _dot(bot, w1_ref[kv, 1])
        hid = p1 + pltpu.roll(p2, p2.shape[0] - 1, 0)
        out = _dot(jax.nn.gelu(hid).astype(BF16), w2_ref[kv]).astype(o_ref.dtype)
        o_ref[kv] = jnp.concatenate([out, out], axis=1)


def _cmp(xc, pos, w1, w2):
    nb = xc.shape[0]
    nch, cw = xc.shape[2], xc.shape[3]
    full = lambda a: pl.BlockSpec(a.shape, lambda i: (0,) * a.ndim)
    return pl.pallas_call(
        _cmp_kernel,
        out_shape=jax.ShapeDtypeStruct((nb, 2, nch, LANES), BF16),
        grid=(nb,),
        in_specs=[pl.BlockSpec((None, 2, nch, cw), lambda i: (i, 0, 0, 0)),
                  full(pos), full(w1), full(w2)],
        out_specs=pl.BlockSpec((None, 2, nch, LANES), lambda i: (i, 0, 0, 0)),
        compiler_params=_cparams(("parallel",)),
        name="cmp",
    )(xc, pos, w1, w2)


N_SEL_TILES = 20
N_WIN_TILES = 9
NSA_CHAIN_HEADS = 2


def _nsa_kernel(q_ref, kv_ref, gate_ref, kcvc_ref, bsel_ref, bwin_ref, et_ref, ov_ref, o_ref):
    i = pl.program_id(2)
    seq = kv_ref.shape[0]
    rows = lax.broadcasted_iota(jnp.int32, (TQ, 1), 0)
    t = i * TQ + rows
    q = q_ref[...]
    qs = _stack_halves([q[:, p * LANES:(p + 1) * LANES] for p in range(B_HPG // 2)])
    gates = jax.nn.sigmoid(gate_ref[...].astype(F32))
    head = lambda a, h: a[h * TQ:(h + 1) * TQ]

    kc = kcvc_ref[0]
    vc = kcvc_ref[1]
    n_cmp = kc.shape[0]
    n_idx = lax.broadcasted_iota(jnp.int32, (1, n_cmp), 1)
    cmask = (n_idx * CMP_STRIDE + (CMP_LEN - 1)) <= t
    c_add = jnp.concatenate([jnp.where(cmask, 0.0, NEG)] * B_HPG, axis=0)
    c_mul = jnp.concatenate([cmask.astype(F32)] * B_HPG, axis=0)
    lc = _dot_nt(qs, kc) + c_add
    e = jnp.exp2(lc - jnp.max(lc, axis=-1, keepdims=True)) * c_mul
    p_cmp = e / jnp.maximum(jnp.sum(e, axis=-1, keepdims=True), TINY)
    o_cmp = _dot(p_cmp.astype(BF16), vc)
    psum = head(p_cmp, 0)
    for h in range(1, B_HPG):
        psum = psum + head(p_cmp, h)

    hi = psum.astype(BF16)
    lo = (psum - hi.astype(F32)).astype(BF16)
    score = _dot(hi, ov_ref[...]) + _dot(lo, ov_ref[...])
    n_sel = seq // SEL_LEN
    jb = lax.broadcasted_iota(jnp.int32, (1, LANES), 1)
    back = lax.shift_right_arithmetic(t, int(math.log2(SEL_LEN))) - jb
    forced = (jb == 0) | ((back >= 0) & (back < SEL_LOCAL))
    score = jnp.where(forced, FORCED_SCORE, score)
    score = jnp.where(jb * SEL_LEN <= t, score, NEG)
    rank = jnp.zeros((TQ, LANES), jnp.int32)
    for ii in range(n_sel):
        col = score[:, ii:ii + 1]
        beats = (col > score) | ((col == score) & (ii < jb))
        rank = rank + beats.astype(jnp.int32)
    sel_neg = jnp.where(rank < SEL_TOPK, 0.0, NEG).astype(BF16)
    qs_sel = jnp.concatenate([qs, jnp.concatenate([sel_neg] * B_HPG, axis=0)], axis=1)

    n_wt = WIN // TQ + 1
    kt0 = jnp.maximum(i - WIN // TQ, 0)
    k0 = pl.multiple_of(kt0 * TQ, TQ)
    w0 = WIN // TQ - i + kt0
    k_w = kv_ref[pl.ds(k0, n_wt * TQ), LANES:2 * LANES]
    v_w = kv_ref[pl.ds(k0, n_wt * TQ), 3 * LANES:4 * LANES]

    def chains(q_all, k, v, bias_ref, first_tile, n_tiles):
        outs = []
        for h0 in range(0, B_HPG, NSA_CHAIN_HEADS):
            bias = jnp.concatenate(
                [jnp.concatenate([bias_ref[h, first_tile + kk] for kk in range(n_tiles)], axis=1)
                 for h in range(h0, h0 + NSA_CHAIN_HEADS)], axis=0)
            outs.append(_softmax_pv(_dot_nt(q_all[h0 * TQ:(h0 + NSA_CHAIN_HEADS) * TQ], k) + bias, v))
        return jnp.concatenate(outs, axis=0)

    o_win = chains(qs, k_w, v_w, bwin_ref, w0, n_wt)

    def finish(o_sel):
        low = _lane_half((TQ, LANES))
        outs = []
        for h in range(B_HPG):
            outs.append(gates[:, h:h + 1] * head(o_cmp, h)
                        + gates[:, B_HPG + h:B_HPG + h + 1] * head(o_sel, h)
                        + gates[:, 2 * B_HPG + h:2 * B_HPG + h + 1] * head(o_win, h))
        for p in range(B_HPG // 2):
            o_ref[:, p * LANES:(p + 1) * LANES] = jnp.where(low, outs[2 * p], outs[2 * p + 1]).astype(o_ref.dtype)

    tiles_per_ext = EXT_STEP // TQ
    for v in range(seq // EXT_STEP):
        @pl.when(i // tiles_per_ext == v)
        def _(v=v):
            ext = (v + 1) * EXT_STEP
            ks = jnp.concatenate([kv_ref[0:ext, 0:LANES], et_ref[0:ext, :]], axis=1)
            b0 = (N_SEL_TILES - tiles_per_ext - 1) - i
            finish(chains(qs_sel, ks, kv_ref[0:ext, 2 * LANES:3 * LANES], bsel_ref, b0, ext // TQ))


def _nsa(p3, kcvc, bsel, bwin, et_mat, ov_mat):
    b, s, _ = p3.shape
    nq = s // TQ
    qw = B_HPG * HEAD_DIM
    kvw = BKV_UNITS * LANES
    full = lambda a: pl.BlockSpec(a.shape, lambda g, bb, i: (0,) * a.ndim)
    per_group = lambda a: pl.BlockSpec((B_HPG,) + a.shape[1:], lambda g, bb, i: (g, 0, 0, 0),
                                       pipeline_mode=pl.Buffered(1))
    return pl.pallas_call(
        _nsa_kernel,
        out_shape=jax.ShapeDtypeStruct((b, s, B_HEADS * HEAD_DIM), BF16),
        grid=(B_KV_GROUPS, b, nq),
        in_specs=[
            pl.BlockSpec((None, TQ, qw), lambda g, bb, i: (bb, i, U_BQ * LANES // qw + g)),
            pl.BlockSpec((None, s, kvw), lambda g, bb, i: (bb, 0, U_BKV * LANES // kvw + g)),
            pl.BlockSpec((None, TQ, LANES), lambda g, bb, i: (bb, i, U_BGATE + g)),
            pl.BlockSpec((None, 2, kcvc.shape[2], LANES), lambda g, bb, i: (bb * B_KV_GROUPS + g, 0, 0, 0)),
            per_group(bsel), per_group(bwin), full(et_mat), full(ov_mat)],
        out_specs=pl.BlockSpec((None, TQ, qw), lambda g, bb, i: (bb, i, g)),
        compiler_params=_cparams(("parallel", "parallel", "arbitrary")),
        name="nsa",
    )(p3, p3, p3, kcvc, bsel, bwin, et_mat, ov_mat)


N_C_TILES = 18


def _diff_kernel(q_ref, k_ref, v_ref, bias_ref, lam_ref, gain_ref, o_ref, *, lam_init):
    i = pl.program_id(2)
    seq = k_ref.shape[0]
    lv = lam_ref[...]
    lam = (jnp.exp(jnp.sum(lv[0:1] * lv[1:2], axis=-1, keepdims=True))
           - jnp.exp(jnp.sum(lv[2:3] * lv[3:4], axis=-1, keepdims=True)) + lam_init)
    qs = _stack_halves([q_ref[...]])
    q_per_ext = EXT_STEP // TQC
    for v in range(seq // EXT_STEP):
        @pl.when(i // q_per_ext == v)
        def _(v=v):
            ext = (v + 1) * EXT_STEP
            b0 = (N_C_TILES - EXT_STEP // TQ) - 2 * i
            bias = jnp.concatenate([bias_ref[b0 + kk] for kk in range(ext // TQ)], axis=1)
            o1 = _softmax_pv(_dot_nt(qs[:TQC], k_ref[0:ext, :]) + bias, v_ref[0:ext, :])
            o2 = _softmax_pv(_dot_nt(qs[TQC:], k_ref[0:ext, :]) + bias, v_ref[0:ext, :])
            o = o1 - lam * o2
            o_ref[...] = (_rms(o, gain_ref[...]) * (1.0 - lam_init)).astype(o_ref.dtype)


def _diff(p3, bias, lam_vecs, sub_gain, lam_init):
    b, s, _ = p3.shape
    dv = 2 * HEAD_DIM
    full = lambda a: pl.BlockSpec(a.shape, lambda h, bb, i: (0,) * a.ndim)
    return pl.pallas_call(
        functools.partial(_diff_kernel, lam_init=lam_init),
        out_shape=jax.ShapeDtypeStruct((b, s, C_HEADS * dv), BF16),
        grid=(C_HEADS, b, s // TQC),
        in_specs=[
            pl.BlockSpec((None, TQC, dv), lambda h, bb, i: (bb, i, U_C + h)),
            pl.BlockSpec((None, s, dv), lambda h, bb, i: (bb, 0, U_C + C_HEADS + h)),
            pl.BlockSpec((None, s, dv), lambda h, bb, i: (bb, 0, U_C + 2 * C_HEADS + h)),
            pl.BlockSpec((None,) + bias.shape[1:], lambda h, bb, i: (h, 0, 0, 0)),
            full(lam_vecs), full(sub_gain)],
        out_specs=pl.BlockSpec((None, TQC, dv), lambda h, bb, i: (bb, i, h)),
        compiler_params=_cparams(("parallel", "parallel", "arbitrary")),
        name="diff",
    )(p3, p3, p3, bias, lam_vecs, sub_gain)


def _mixout_kernel(x_ref, oa0_ref, oa1_ref, oa2_ref, l0_ref, l1_ref, l2_ref, ob_ref, oc_ref,
                   g0_ref, g1_ref, g2_ref, wa_ref, wb_ref, wc_ref, wo_ref, gain_ref, o_ref):
    planes = lambda ref: jnp.concatenate([ref[p] for p in range(ref.shape[0])], axis=1)
    lse = [planes(l0_ref), planes(l1_ref), planes(l2_ref)]
    m = jnp.maximum(jnp.maximum(lse[0], lse[1]), lse[2])
    e = [jnp.exp(v - m) for v in lse]
    den = e[0] + e[1] + e[2]
    oa = (e[0] / den) * planes(oa0_ref) + (e[1] / den) * planes(oa1_ref) + (e[2] / den) * planes(oa2_ref)
    mixed = (jax.nn.sigmoid(g0_ref[...].astype(F32)) * _dot(oa.astype(BF16), wa_ref[...])
             + jax.nn.sigmoid(g1_ref[...].astype(F32)) * _dot(ob_ref[...], wb_ref[...])
             + jax.nn.sigmoid(g2_ref[...].astype(F32)) * _dot(oc_ref[...], wc_ref[...]))
    y = _dot(mixed.astype(BF16), wo_ref[...])
    o_ref[...] = x_ref[...] + _rms(y, gain_ref[...])


def _mixout(x, oa, lse, ob, oc, p, wa, wb, wc, wo, gain, tm=256):
    n, d = x.shape
    const = lambda a: pl.BlockSpec(a.shape, lambda i: (0,) * a.ndim, pipeline_mode=pl.Buffered(1))
    row = lambda a: pl.BlockSpec((tm, a.shape[1]), lambda i: (i, 0))
    grp = lambda a: pl.BlockSpec((a.shape[0], tm, a.shape[2]), lambda i: (0, i, 0))
    gate = lambda br: pl.BlockSpec((tm, d), lambda i: (i, U_MGATE * LANES // d + br))
    return pl.pallas_call(
        _mixout_kernel,
        out_shape=jax.ShapeDtypeStruct((n, d), F32),
        grid=(n // tm,),
        in_specs=[row(x)] + [grp(a) for a in oa] + [grp(a) for a in lse] + [row(ob), row(oc),
                  gate(0), gate(1), gate(2), const(wa), const(wb), const(wc), const(wo), const(gain)],
        out_specs=row(x),
        compiler_params=_cparams(("parallel",)),
        name="mixout",
    )(x, *oa, *lse, ob, oc, p, p, p, wa, wb, wc, wo, gain)


def _mlp_kernel(x_ref, gpre_ref, wup_ref, wdn_ref, gpost_ref, o_ref, h_ref, acc_ref):
    f = pl.program_id(1)

    @pl.when(f == 0)
    def _():
        h_ref[...] = _rms(x_ref[...], gpre_ref[...]).astype(BF16)
        acc_ref[...] = jnp.zeros_like(acc_ref)

    u = jnp.square(jax.nn.relu(_dot(h_ref[...], wup_ref[...])))
    acc_ref[...] += _dot(u.astype(BF16), wdn_ref[...])

    @pl.when(f == pl.num_programs(1) - 1)
    def _():
        o_ref[...] = x_ref[...] + _rms(acc_ref[...], gpost_ref[...])


def _mlp(x, gpre, wup, wdn, gpost, tm=512, tf=1024):
    n, d = x.shape
    dff = wup.shape[1]
    return pl.pallas_call(
        _mlp_kernel,
        out_shape=jax.ShapeDtypeStruct((n, d), F32),
        grid=(n // tm, dff // tf),
        in_specs=[pl.BlockSpec((tm, d), lambda i, f: (i, 0)),
                  pl.BlockSpec((1, d), lambda i, f: (0, 0)),
                  pl.BlockSpec((d, tf), lambda i, f: (0, f)),
                  pl.BlockSpec((tf, d), lambda i, f: (f, 0)),
                  pl.BlockSpec((1, d), lambda i, f: (0, 0))],
        out_specs=pl.BlockSpec((tm, d), lambda i, f: (i, 0)),
        scratch_shapes=[pltpu.VMEM((tm, d), BF16), pltpu.VMEM((tm, d), F32)],
        compiler_params=_cparams(("parallel", "arbitrary")),
        name="mlp",
    )(x, gpre, wup, wdn, gpost)


def _toeplitz_kernel(v_ref, o_ref):
    n_tiles, rows, _ = o_ref.shape
    x = jnp.broadcast_to(v_ref[...], (rows, v_ref.shape[-1]))
    t = pltpu.roll(x, 0, 1, stride=1, stride_axis=0)
    for k in range(n_tiles):
        o_ref[k] = t[:, k * TQ:(k + 1) * TQ]


def _toeplitz_tiles(per_dist, dist_of_m, rows, n_tiles):
    width = n_tiles * TQ
    w = width + rows
    m = np.arange(w)
    m = np.where(m >= width, m - w, m)
    dist = dist_of_m(m)
    neg_col = per_dist.shape[1]
    idx = np.where(dist >= 0, np.minimum(dist, neg_col - 1), neg_col)
    heads = per_dist.shape[0]
    vec = jnp.concatenate([per_dist, jnp.full((heads, 1), NEG, F32)], axis=1)[:, idx]
    return pl.pallas_call(
        _toeplitz_kernel,
        out_shape=jax.ShapeDtypeStruct((heads, n_tiles, rows, TQ), F32),
        grid=(heads,),
        in_specs=[pl.BlockSpec((None, 1, w), lambda h: (h, 0, 0))],
        out_specs=pl.BlockSpec((None, n_tiles, rows, TQ), lambda h: (h, 0, 0, 0)),
        compiler_params=_cparams(("parallel",)),
        name="toeplitz",
    )(vec.reshape(heads, 1, w))


def _t5_bucket(dist):
    exact = N_BUCKETS // 2
    d = jnp.maximum(dist, 0)
    logd = jnp.log(jnp.maximum(d, 1).astype(F32) / exact)
    far = exact + (logd / math.log(BIAS_MAX_DIST / exact) * (N_BUCKETS - exact)).astype(jnp.int32)
    return jnp.where(d < exact, d, jnp.minimum(far, N_BUCKETS - 1))


def _bias_tables(rel_bias, seq):
    per_dist = rel_bias[_t5_bucket(jnp.arange(seq + 1))].T * LOG2E
    n_a = len(DIL_GROUPS) * A_HEADS
    bias_a = []
    for g, (_, dil) in enumerate(DIL_GROUPS):
        t = _toeplitz_tiles(per_dist[g * A_HEADS:(g + 1) * A_HEADS],
                            lambda m, dil=dil: np.where((m >= 0) & (m <= TQ), dil * (TQ - m), -1), TQ, 2)
        if seq // (TQ * dil) > 1:
            with_prev = t.transpose(0, 2, 1, 3).reshape(A_HEADS * TQ, 2 * TQ)
            no_prev = jnp.where(np.arange(2 * TQ)[None, :] < TQ, NEG, with_prev)
            bias_a.append(jnp.stack([no_prev, with_prev]))
        else:
            bias_a.append(t[:, 1].reshape(A_HEADS * TQ, TQ))
    pb = per_dist[n_a:n_a + B_HEADS]
    off_s = (N_SEL_TILES - EXT_STEP // TQ - 1) * TQ
    bias_sel = _toeplitz_tiles(pb, lambda m: off_s - m, TQ, N_SEL_TILES)
    bias_win = _toeplitz_tiles(pb, lambda m: np.where(WIN - m < WIN, WIN - m, -1), TQ, N_WIN_TILES)
    off_c = (N_C_TILES - EXT_STEP // TQ) * TQ
    bias_c = _toeplitz_tiles(per_dist[n_a + B_HEADS:], lambda m: off_c - m, TQC, N_C_TILES)
    return bias_a, bias_sel, bias_win, bias_c


def _pack_w_in(w):
    d = w.shape[0]
    a_w = A_UNITS * LANES
    aq_w = a_w // 3
    bq_w = B_HEADS * HEAD_DIM
    bkv_w = 6 * B_KV_GROUPS * HEAD_DIM
    bg_w = 3 * B_HEADS
    c_w = 3 * C_HEADS * 2 * HEAD_DIM
    cq_w = c_w // 3
    o_bq, o_bkv, o_bg, o_c, o_mg = a_w, a_w + bq_w, a_w + bq_w + bkv_w, a_w + bq_w + bkv_w + bg_w, \
        a_w + bq_w + bkv_w + bg_w + c_w
    colscale = np.ones((w.shape[1],), np.float32)
    for lo, n_q in ((0, aq_w), (o_bq, bq_w), (o_c, cq_w)):
        colscale[lo:lo + n_q] = SCALE * LOG2E
    wb = (w * colscale).astype(BF16)
    a, bq, bkv, bg, cc, mg = (wb[:, :a_w], wb[:, o_bq:o_bkv], wb[:, o_bkv:o_bg], wb[:, o_bg:o_c],
                              wb[:, o_c:o_mg], wb[:, o_mg:])
    bkv = bkv.reshape(d, 6, B_KV_GROUPS, HEAD_DIM)
    order = (2, 2, 4, 4, 3, 3, 5, 5, 0, 1)
    bkv = jnp.stack([bkv[:, o_, g] for g in range(B_KV_GROUPS) for o_ in order], axis=1)
    bkv = bkv.reshape(d, B_KV_GROUPS * BKV_UNITS * LANES)
    bg = bg.reshape(d, B_KV_GROUPS, B_HPG, 3).transpose(0, 1, 3, 2).reshape(d, B_KV_GROUPS, 3 * B_HPG)
    bg = jnp.pad(bg, ((0, 0), (0, 0), (0, LANES - 3 * B_HPG))).reshape(d, B_KV_GROUPS * LANES)
    packed = jnp.concatenate([a, mg, bq, cc, bkv, bg], axis=1)
    assert packed.shape[1] == a_w + NCOL
    return packed


def kernel(x, rel_bias, w_in, cmp_k_pos, cmp_v_pos, cmp_k_w1, cmp_k_w2, cmp_v_w1, cmp_v_w2, diff_lambda, diff_norm, w_branch_a, w_branch_b, w_branch_c, w_out, norm_mix_pre, norm_mix_post, norm_mlp_pre, norm_mlp_post, w_up, w_down):
    bsz, seq, d = x.shape
    n = bsz * seq
    depth = w_in.shape[0]
    nch = seq // CMP_STRIDE
    assert seq % EXT_STEP == 0 and seq // TQ == N_SEL_TILES - EXT_STEP // TQ and (U_MGATE * LANES) % d == 0

    bias_a, bias_sel, bias_win, bias_c = _bias_tables(rel_bias, seq)
    et_mat = jnp.asarray(np.arange(seq)[:, None] // SEL_LEN == np.arange(LANES)[None, :], BF16)
    starts = np.arange(nch)[:, None] * CMP_STRIDE
    bstart = np.arange(LANES)[None, :] * SEL_LEN
    ov_mat = jnp.asarray((starts < bstart + SEL_LEN) & (starts + CMP_LEN > bstart), BF16)

    xf = x.reshape(n, d)
    for l in range(depth):
        lam_init = 0.8 - 0.6 * math.exp(-0.3 * l)
        pa, p = _proj(xf, norm_mix_pre[l][None], _pack_w_in(w_in[l]))
        p3 = p.reshape(bsz, seq, NCOL)

        a_out = [_attn_a(pa, bias_a[g], g, dil, bsz, seq) for g, (_, dil) in enumerate(DIL_GROUPS)]
        oa = [o for o, _ in a_out]
        lse = [s for _, s in a_out]

        kvc = jnp.stack([p3[:, :, (U_BKV + g * BKV_UNITS + BKV_UNITS - 1) * LANES:(U_BKV + (g + 1) * BKV_UNITS) * LANES]
                         for g in range(B_KV_GROUPS)], axis=1)
        xc = kvc.reshape(bsz, B_KV_GROUPS, seq, 2, HEAD_DIM).transpose(0, 1, 3, 2, 4)
        xc = xc.reshape(bsz * B_KV_GROUPS, 2, nch, CMP_STRIDE * HEAD_DIM)
        half = CMP_STRIDE * HEAD_DIM
        pos = jnp.stack([cmp_k_pos[l], cmp_v_pos[l]]).reshape(2, 2, half)
        w1 = jnp.stack([cmp_k_w1[l], cmp_v_w1[l]]).reshape(2, 2, half, CMP_HIDDEN).astype(BF16)
        w2 = jnp.stack([cmp_k_w2[l], cmp_v_w2[l]]).astype(BF16)
        kcvc = _cmp(xc, pos, w1, w2)
        ob = _nsa(p3, kcvc, bias_sel, bias_win, et_mat, ov_mat).reshape(n, B_HEADS * HEAD_DIM)

        oc = _diff(p3, bias_c, diff_lambda[l], diff_norm[l][None], lam_init).reshape(n, C_HEADS * 2 * HEAD_DIM)

        xf = _mixout(xf, oa, lse, ob, oc, p,
                     w_branch_a[l].astype(BF16), w_branch_b[l].astype(BF16), w_branch_c[l].astype(BF16),
                     w_out[l].astype(BF16), norm_mix_post[l][None])
        xf = _mlp(xf, norm_mlp_pre[l][None], w_up[l].astype(BF16), w_down[l].astype(BF16),
                  norm_mlp_post[l][None])
    return xf.reshape(bsz, seq, d)
```

```python
import functools
import math

import jax
import jax.numpy as jnp
import numpy as np
from jax import lax
from jax.experimental import pallas as pl
from jax.experimental.pallas import tpu as pltpu

F32 = jnp.float32
BF16 = jnp.bfloat16

LANES = 128
HEAD_DIM = 64
SCALE = HEAD_DIM ** -0.5
LOG2E = math.log2(math.e)
LN2 = math.log(2.0)
NORM_EPS = 1e-6
NEG = -1e30
TINY = 1e-30
TQ = 128
TQC = 512
EXT_STEP = 512

DIL_GROUPS = ((128, 1), (512, 4), (2048, 16))
A_HEADS = 8
A_PAIRS = A_HEADS // 2
B_HEADS = 16
B_KV_GROUPS = 2
B_HPG = B_HEADS // B_KV_GROUPS
C_HEADS = 8
CMP_LEN, CMP_STRIDE, CMP_HIDDEN = 32, 16, 256
SEL_LEN, SEL_TOPK, SEL_LOCAL = 64, 16, 2
FORCED_SCORE = 1e6
WIN = 512
N_BUCKETS = 32
BIAS_MAX_DIST = 2048

A_UNITS = 3 * len(DIL_GROUPS) * A_PAIRS
U_MGATE = 0
U_BQ = 48
U_C = 56
U_BKV = 80
BKV_UNITS = 5
U_BGATE = 90
U_TOTAL = 92
NCOL = U_TOTAL * LANES

VMEM_LIMIT = 56 * 1024 * 1024


def _cparams(sem):
    return pltpu.CompilerParams(dimension_semantics=sem, vmem_limit_bytes=VMEM_LIMIT)


def _rms(x, gain):
    return x * lax.rsqrt(jnp.mean(x * x, axis=-1, keepdims=True) + NORM_EPS) * gain


def _dot_nt(a, b):
    return lax.dot_general(a, b, (((1,), (1,)), ((), ())), preferred_element_type=F32)


def _dot(a, b):
    return jnp.dot(a, b, preferred_element_type=F32)


def _lane_half(shape):
    return lax.broadcasted_iota(jnp.int32, shape, len(shape) - 1) < HEAD_DIM


def _stack_halves(tiles):
    low = _lane_half(tiles[0].shape)
    out = []
    for t in tiles:
        z = jnp.zeros_like(t)
        out += [jnp.where(low, t, z), jnp.where(low, z, t)]
    return jnp.concatenate(out, axis=0)


def _softmax_pv(s, v):
    m = jnp.max(s, axis=-1, keepdims=True)
    p = jnp.exp2(s - m)
    l = jnp.sum(p, axis=-1, keepdims=True)
    return _dot(p.astype(BF16), v) / l


def _proj_kernel(x_ref, g_ref, w_ref, oa_ref, o_ref, h_ref, *, n_a):
    j = pl.program_id(1)

    @pl.when(j == 0)
    def _():
        h_ref[...] = _rms(x_ref[...], g_ref[...]).astype(BF16)

    res = _dot(h_ref[...], w_ref[...])

    @pl.when(j < n_a)
    def _():
        for c in range(oa_ref.shape[0]):
            oa_ref[c] = res[:, c * LANES:(c + 1) * LANES]

    @pl.when(j >= n_a)
    def _():
        o_ref[...] = res.astype(o_ref.dtype)


def _proj(x, gain, w, tm=2048, tn=512):
    n, d = x.shape
    n_a = A_UNITS * LANES // tn
    upb = tn // LANES
    return pl.pallas_call(
        functools.partial(_proj_kernel, n_a=n_a),
        out_shape=(jax.ShapeDtypeStruct((A_UNITS, n, LANES), F32), jax.ShapeDtypeStruct((n, NCOL), BF16)),
        grid=(n // tm, w.shape[1] // tn),
        in_specs=[pl.BlockSpec((tm, d), lambda i, j: (i, 0), pipeline_mode=pl.Buffered(1)),
                  pl.BlockSpec((1, d), lambda i, j: (0, 0)),
                  pl.BlockSpec((d, tn), lambda i, j: (0, j))],
        out_specs=(pl.BlockSpec((upb, tm, LANES), lambda i, j: (jnp.minimum(j, n_a - 1), i, 0)),
                   pl.BlockSpec((tm, tn), lambda i, j: (i, jnp.maximum(j - n_a, 0)))),
        scratch_shapes=[pltpu.VMEM((tm, d), BF16)],
        compiler_params=_cparams(("parallel", "arbitrary")),
        name="proj",
    )(x, gain, w)


def _attn_a_kernel(*refs, dil, use_prev):
    if use_prev:
        q_ref, kc_ref, vc_ref, kp_ref, vp_ref, bias_ref, o_ref, lse_ref = refs
    else:
        q_ref, kc_ref, vc_ref, bias_ref, o_ref, lse_ref = refs
    sb = pl.program_id(1)
    r = pl.program_id(2)
    rows = pl.ds(r, TQ, stride=dil) if dil > 1 else slice(None)
    ld = lambda ref, p: ref[p, rows, :].astype(BF16)
    s = []
    for p in range(A_PAIRS):
        qs = _stack_halves([ld(q_ref, p)])
        ks = jnp.concatenate([ld(kp_ref, p), ld(kc_ref, p)], axis=0) if use_prev else ld(kc_ref, p)
        s.append(_dot_nt(qs, ks))
    bias = bias_ref[(sb > 0).astype(jnp.int32)] if use_prev else bias_ref[...]
    s = jnp.concatenate(s, axis=0) + bias
    m = jnp.max(s, axis=-1, keepdims=True)
    e = jnp.exp2(s - m)
    l = jnp.sum(e, axis=-1, keepdims=True)
    pb = e.astype(BF16)
    lse = (m + jnp.log2(l)) * LN2
    low = _lane_half((TQ, LANES))
    for p in range(A_PAIRS):
        r0 = 2 * p * TQ
        vs = jnp.concatenate([ld(vp_ref, p), ld(vc_ref, p)], axis=0) if use_prev else ld(vc_ref, p)
        o = _dot(pb[r0:r0 + 2 * TQ], vs) / l[r0:r0 + 2 * TQ]
        o_ref[p, rows, :] = jnp.where(low, o[:TQ], o[TQ:])
        lse_ref[p, rows, :] = jnp.where(low, lse[r0:r0 + TQ], lse[r0 + TQ:r0 + 2 * TQ])


def _attn_a(pa, bias, g, dil, bsz, seq):
    n = pa.shape[1]
    blk = TQ * dil
    nsb = seq // blk
    use_prev = nsb > 1
    n_grp = len(DIL_GROUPS)
    cur = lambda qkv: pl.BlockSpec((A_PAIRS, blk, LANES), lambda b, sb, r: (qkv * n_grp + g, b * nsb + sb, 0))
    prev = lambda qkv: pl.BlockSpec((A_PAIRS, blk, LANES),
                                    lambda b, sb, r: (qkv * n_grp + g, b * nsb + jnp.maximum(sb - 1, 0), 0))
    out = pl.BlockSpec((A_PAIRS, blk, LANES), lambda b, sb, r: (0, b * nsb + sb, 0))
    in_specs = [cur(0), cur(1), cur(2)] + ([prev(1), prev(2)] if use_prev else [])
    in_specs.append(pl.BlockSpec(bias.shape, lambda b, sb, r: (0,) * bias.ndim))
    args = [pa] * (len(in_specs) - 1) + [bias]
    return pl.pallas_call(
        functools.partial(_attn_a_kernel, dil=dil, use_prev=use_prev),
        out_shape=(jax.ShapeDtypeStruct((A_PAIRS, n, LANES), F32),) * 2,
        grid=(bsz, nsb, dil),
        in_specs=in_specs,
        out_specs=(out, out),
        compiler_params=_cparams(("parallel", "parallel", "arbitrary")),
        name=f"attn_a_d{dil}",
    )(*args)


def _cmp_kernel(x_ref, pos_ref, w1_ref, w2_ref, o_ref):
    for kv in range(2):
        x = x_ref[kv].astype(F32)
        top = (x + pos_ref[kv, 0:1, :]).astype(BF16)
        bot = (x + pos_ref[kv, 1:2, :]).astype(BF16)
        p1 = _dot(top, w1_ref[kv, 0])
        p2 = _dot(bot, w1_ref[kv, 1])
        hid = p1 + pltpu.roll(p2, p2.shape[0] - 1, 0)
        out = _dot(jax.nn.gelu(hid).astype(BF16), w2_ref[kv]).astype(o_ref.dtype)
        o_ref[kv] = jnp.concatenate([out, out], axis=1)


def _cmp(xc, pos, w1, w2):
    nb = xc.shape[0]
    nch, cw = xc.shape[2], xc.shape[3]
    full = lambda a: pl.BlockSpec(a.shape, lambda i: (0,) * a.ndim)
    return pl.pallas_call(
        _cmp_kernel,
        out_shape=jax.ShapeDtypeStruct((nb, 2, nch, LANES), BF16),
        grid=(nb,),
        in_specs=[pl.BlockSpec((None, 2, nch, cw), lambda i: (i, 0, 0, 0)),
                  full(pos), full(w1), full(w2)],
        out_specs=pl.BlockSpec((None, 2, nch, LANES), lambda i: (i, 0, 0, 0)),
        compiler_params=_cparams(("parallel",)),
        name="cmp",
    )(xc, pos, w1, w2)


N_SEL_TILES = 20
N_WIN_TILES = 9
NSA_CHAIN_HEADS = 2


def _nsa_kernel(q_ref, kv_ref, gate_ref, kcvc_ref, bsel_ref, bwin_ref, et_ref, ov_ref, o_ref):
    i = pl.program_id(2)
    seq = kv_ref.shape[0]
    rows = lax.broadcasted_iota(jnp.int32, (TQ, 1), 0)
    t = i * TQ + rows
    q = q_ref[...]
    qs = _stack_halves([q[:, p * LANES:(p + 1) * LANES] for p in range(B_HPG // 2)])
    gates = jax.nn.sigmoid(gate_ref[...].astype(F32))
    head = lambda a, h: a[h * TQ:(h + 1) * TQ]

    kc = kcvc_ref[0]
    vc = kcvc_ref[1]
    n_cmp = kc.shape[0]
    n_idx = lax.broadcasted_iota(jnp.int32, (1, n_cmp), 1)
    cmask = (n_idx * CMP_STRIDE + (CMP_LEN - 1)) <= t
    c_add = jnp.concatenate([jnp.where(cmask, 0.0, NEG)] * B_HPG, axis=0)
    c_mul = jnp.concatenate([cmask.astype(F32)] * B_HPG, axis=0)
    lc = _dot_nt(qs, kc) + c_add
    e = jnp.exp2(lc - jnp.max(lc, axis=-1, keepdims=True)) * c_mul
    p_cmp = e / jnp.maximum(jnp.sum(e, axis=-1, keepdims=True), TINY)
    o_cmp = _dot(p_cmp.astype(BF16), vc)
    psum = head(p_cmp, 0)
    for h in range(1, B_HPG):
        psum = psum + head(p_cmp, h)

    hi = psum.astype(BF16)
    lo = (psum - hi.astype(F32)).astype(BF16)
    score = _dot(hi, ov_ref[...]) + _dot(lo, ov_ref[...])
    n_sel = seq // SEL_LEN
    jb = lax.broadcasted_iota(jnp.int32, (1, LANES), 1)
    back = lax.shift_right_arithmetic(t, int(math.log2(SEL_LEN))) - jb
    forced = (jb == 0) | ((back >= 0) & (back < SEL_LOCAL))
    score = jnp.where(forced, FORCED_SCORE, score)
    score = jnp.where(jb * SEL_LEN <= t, score, NEG)
    rank = jnp.zeros((TQ, LANES), jnp.int32)
    for ii in range(n_sel):
        col = score[:, ii:ii + 1]
        beats = (col > score) | ((col == score) & (ii < jb))
        rank = rank + beats.astype(jnp.int32)
    sel_neg = jnp.where(rank < SEL_TOPK, 0.0, NEG).astype(BF16)
    qs_sel = jnp.concatenate([qs, jnp.concatenate([sel_neg] * B_HPG, axis=0)], axis=1)

    n_wt = WIN // TQ + 1
    kt0 = jnp.maximum(i - WIN // TQ, 0)
    k0 = pl.multiple_of(kt0 * TQ, TQ)
    w0 = WIN // TQ - i + kt0
    k_w = kv_ref[pl.ds(k0, n_wt * TQ), LANES:2 * LANES]
    v_w = kv_ref[pl.ds(k0, n_wt * TQ), 3 * LANES:4 * LANES]

    def chains(q_all, k, v, bias_ref, first_tile, n_tiles):
        outs = []
        for h0 in range(0, B_HPG, NSA_CHAIN_HEADS):
            bias = jnp.concatenate(
                [jnp.concatenate([bias_ref[h, first_tile + kk] for kk in range(n_tiles)], axis=1)
                 for h in range(h0, h0 + NSA_CHAIN_HEADS)], axis=0)
            outs.append(_softmax_pv(_dot_nt(q_all[h0 * TQ:(h0 + NSA_CHAIN_HEADS) * TQ], k) + bias, v))
        return jnp.concatenate(outs, axis=0)

    o_win = chains(qs, k_w, v_w, bwin_ref, w0, n_wt)

    def finish(o_sel):
        low = _lane_half((TQ, LANES))
        outs = []
        for h in range(B_HPG):
            outs.append(gates[:, h:h + 1] * head(o_cmp, h)
                        + gates[:, B_HPG + h:B_HPG + h + 1] * head(o_sel, h)
                        + gates[:, 2 * B_HPG + h:2 * B_HPG + h + 1] * head(o_win, h))
        for p in range(B_HPG // 2):
            o_ref[:, p * LANES:(p + 1) * LANES] = jnp.where(low, outs[2 * p], outs[2 * p + 1]).astype(o_ref.dtype)

    tiles_per_ext = EXT_STEP // TQ
    for v in range(seq // EXT_STEP):
        @pl.when(i // tiles_per_ext == v)
        def _(v=v):
            ext = (v + 1) * EXT_STEP
            ks = jnp.concatenate([kv_ref[0:ext, 0:LANES], et_ref[0:ext, :]], axis=1)
            b0 = (N_SEL_TILES - tiles_per_ext - 1) - i
            finish(chains(qs_sel, ks, kv_ref[0:ext, 2 * LANES:3 * LANES], bsel_ref, b0, ext // TQ))


def _nsa(p3, kcvc, bsel, bwin, et_mat, ov_mat):
    b, s, _ = p3.shape
    nq = s // TQ
    qw = B_HPG * HEAD_DIM
    kvw = BKV_UNITS * LANES
    full = lambda a: pl.BlockSpec(a.shape, lambda g, bb, i: (0,) * a.ndim)
    per_group = lambda a: pl.BlockSpec((B_HPG,) + a.shape[1:], lambda g, bb, i: (g, 0, 0, 0),
                                       pipeline_mode=pl.Buffered(1))
    return pl.pallas_call(
        _nsa_kernel,
        out_shape=jax.ShapeDtypeStruct((b, s, B_HEADS * HEAD_DIM), BF16),
        grid=(B_KV_GROUPS, b, nq),
        in_specs=[
            pl.BlockSpec((None, TQ, qw), lambda g, bb, i: (bb, i, U_BQ * LANES // qw + g)),
            pl.BlockSpec((None, s, kvw), lambda g, bb, i: (bb, 0, U_BKV * LANES // kvw + g)),
            pl.BlockSpec((None, TQ, LANES), lambda g, bb, i: (bb, i, U_BGATE + g)),
            pl.BlockSpec((None, 2, kcvc.shape[2], LANES), lambda g, bb, i: (bb * B_KV_GROUPS + g, 0, 0, 0)),
            per_group(bsel), per_group(bwin), full(et_mat), full(ov_mat)],
        out_specs=pl.BlockSpec((None, TQ, qw), lambda g, bb, i: (bb, i, g)),
        compiler_params=_cparams(("parallel", "parallel", "arbitrary")),
        name="nsa",
    )(p3, p3, p3, kcvc, bsel, bwin, et_mat, ov_mat)


C_TILES_PER_Q = TQC // TQ
DIFF_CHAIN_ROWS = 256
N_C_TILES = 16


def _diff_kernel(q_ref, k_ref, v_ref, bias_ref, lam_ref, gain_ref, o_ref, *, lam_init):
    i = pl.program_id(2)
    seq = k_ref.shape[0]
    lv = lam_ref[...]
    lam = (jnp.exp(jnp.sum(lv[0:1] * lv[1:2], axis=-1, keepdims=True))
           - jnp.exp(jnp.sum(lv[2:3] * lv[3:4], axis=-1, keepdims=True)) + lam_init)
    qs = _stack_halves([q_ref[...]])
    q_per_ext = EXT_STEP // TQC
    for v in range(seq // EXT_STEP):
        @pl.when(i // q_per_ext == v)
        def _(v=v):
            ext = (v + 1) * EXT_STEP
            b0 = (N_C_TILES - EXT_STEP // TQ) - C_TILES_PER_Q * i
            bias = jnp.concatenate([bias_ref[b0 + kk] for kk in range(ext // TQ)], axis=1)
            bias2 = jnp.concatenate([bias, bias], axis=0)
            o = jnp.concatenate(
                [_softmax_pv(_dot_nt(qs[r0:r0 + DIFF_CHAIN_ROWS], k_ref[0:ext, :]) + bias2[r0:r0 + DIFF_CHAIN_ROWS],
                             v_ref[0:ext, :]) for r0 in range(0, 2 * TQC, DIFF_CHAIN_ROWS)], axis=0)
            o = o[:TQC] - lam * o[TQC:]
            o_ref[...] = (_rms(o, gain_ref[...]) * (1.0 - lam_init)).astype(o_ref.dtype)


def _diff(p3, bias, lam_vecs, sub_gain, lam_init):
    b, s, _ = p3.shape
    dv = 2 * HEAD_DIM
    full = lambda a: pl.BlockSpec(a.shape, lambda h, bb, i: (0,) * a.ndim)
    return pl.pallas_call(
        functools.partial(_diff_kernel, lam_init=lam_init),
        out_shape=jax.ShapeDtypeStruct((b, s, C_HEADS * dv), BF16),
        grid=(C_HEADS, b, s // TQC),
        in_specs=[
            pl.BlockSpec((None, TQC, dv), lambda h, bb, i: (bb, i, U_C + h)),
            pl.BlockSpec((None, s, dv), lambda h, bb, i: (bb, 0, U_C + C_HEADS + h)),
            pl.BlockSpec((None, s, dv), lambda h, bb, i: (bb, 0, U_C + 2 * C_HEADS + h)),
            pl.BlockSpec((None,) + bias.shape[1:], lambda h, bb, i: (h, 0, 0, 0)),
            full(lam_vecs), full(sub_gain)],
        out_specs=pl.BlockSpec((None, TQC, dv), lambda h, bb, i: (bb, i, h)),
        compiler_params=_cparams(("parallel", "parallel", "arbitrary")),
        name="diff",
    )(p3, p3, p3, bias, lam_vecs, sub_gain)


def _mixout_kernel(x_ref, oa0_ref, oa1_ref, oa2_ref, l0_ref, l1_ref, l2_ref, ob_ref, oc_ref,
                   g0_ref, g1_ref, g2_ref, wa_ref, wb_ref, wc_ref, wo_ref, gain_ref, o_ref):
    planes = lambda ref: jnp.concatenate([ref[p] for p in range(ref.shape[0])], axis=1)
    lse = [planes(l0_ref), planes(l1_ref), planes(l2_ref)]
    m = jnp.maximum(jnp.maximum(lse[0], lse[1]), lse[2])
    e = [jnp.exp(v - m) for v in lse]
    den = e[0] + e[1] + e[2]
    oa = (e[0] / den) * planes(oa0_ref) + (e[1] / den) * planes(oa1_ref) + (e[2] / den) * planes(oa2_ref)
    mixed = (jax.nn.sigmoid(g0_ref[...].astype(F32)) * _dot(oa.astype(BF16), wa_ref[...])
             + jax.nn.sigmoid(g1_ref[...].astype(F32)) * _dot(ob_ref[...], wb_ref[...])
             + jax.nn.sigmoid(g2_ref[...].astype(F32)) * _dot(oc_ref[...], wc_ref[...]))
    y = _dot(mixed.astype(BF16), wo_ref[...])
    o_ref[...] = x_ref[...] + _rms(y, gain_ref[...])


def _mixout(x, oa, lse, ob, oc, p, wa, wb, wc, wo, gain, tm=256):
    n, d = x.shape
    const = lambda a: pl.BlockSpec(a.shape, lambda i: (0,) * a.ndim, pipeline_mode=pl.Buffered(1))
    row = lambda a: pl.BlockSpec((tm, a.shape[1]), lambda i: (i, 0))
    grp = lambda a: pl.BlockSpec((a.shape[0], tm, a.shape[2]), lambda i: (0, i, 0))
    gate = lambda br: pl.BlockSpec((tm, d), lambda i: (i, U_MGATE * LANES // d + br))
    return pl.pallas_call(
        _mixout_kernel,
        out_shape=jax.ShapeDtypeStruct((n, d), F32),
        grid=(n // tm,),
        in_specs=[row(x)] + [grp(a) for a in oa] + [grp(a) for a in lse] + [row(ob), row(oc),
                  gate(0), gate(1), gate(2), const(wa), const(wb), const(wc), const(wo), const(gain)],
        out_specs=row(x),
        compiler_params=_cparams(("parallel",)),
        name="mixout",
    )(x, *oa, *lse, ob, oc, p, p, p, wa, wb, wc, wo, gain)


def _mlp_kernel(x_ref, gpre_ref, wup_ref, wdn_ref, gpost_ref, o_ref, h_ref, acc_ref):
    f = pl.program_id(1)

    @pl.when(f == 0)
    def _():
        h_ref[...] = _rms(x_ref[...], gpre_ref[...]).astype(BF16)
        acc_ref[...] = jnp.zeros_like(acc_ref)

    u = jnp.square(jax.nn.relu(_dot(h_ref[...], wup_ref[...])))
    acc_ref[...] += _dot(u.astype(BF16), wdn_ref[...])

    @pl.when(f == pl.num_programs(1) - 1)
    def _():
        o_ref[...] = x_ref[...] + _rms(acc_ref[...], gpost_ref[...])


def _mlp(x, gpre, wup, wdn, gpost, tm=1024, tf=512):
    n, d = x.shape
    dff = wup.shape[1]
    return pl.pallas_call(
        _mlp_kernel,
        out_shape=jax.ShapeDtypeStruct((n, d), F32),
        grid=(n // tm, dff // tf),
        in_specs=[pl.BlockSpec((tm, d), lambda i, f: (i, 0), pipeline_mode=pl.Buffered(1)),
                  pl.BlockSpec((1, d), lambda i, f: (0, 0)),
                  pl.BlockSpec((d, tf), lambda i, f: (0, f)),
                  pl.BlockSpec((tf, d), lambda i, f: (f, 0)),
                  pl.BlockSpec((1, d), lambda i, f: (0, 0))],
        out_specs=pl.BlockSpec((tm, d), lambda i, f: (i, 0)),
        scratch_shapes=[pltpu.VMEM((tm, d), BF16), pltpu.VMEM((tm, d), F32)],
        compiler_params=_cparams(("parallel", "arbitrary")),
        name="mlp",
    )(x, gpre, wup, wdn, gpost)


def _toeplitz_kernel(v_ref, o_ref):
    n_tiles, rows, _ = o_ref.shape
    x = jnp.broadcast_to(v_ref[...], (rows, v_ref.shape[-1]))
    t = pltpu.roll(x, 0, 1, stride=1, stride_axis=0)
    for k in range(n_tiles):
        o_ref[k] = t[:, k * TQ:(k + 1) * TQ]


def _toeplitz_tiles(per_dist, dist_of_m, rows, n_tiles):
    width = n_tiles * TQ
    w = width + rows
    m = np.arange(w)
    m = np.where(m >= width, m - w, m)
    dist = dist_of_m(m)
    neg_col = per_dist.shape[1]
    idx = np.where(dist >= 0, np.minimum(dist, neg_col - 1), neg_col)
    heads = per_dist.shape[0]
    vec = jnp.concatenate([per_dist, jnp.full((heads, 1), NEG, F32)], axis=1)[:, idx]
    return pl.pallas_call(
        _toeplitz_kernel,
        out_shape=jax.ShapeDtypeStruct((heads, n_tiles, rows, TQ), F32),
        grid=(heads,),
        in_specs=[pl.BlockSpec((None, 1, w), lambda h: (h, 0, 0))],
        out_specs=pl.BlockSpec((None, n_tiles, rows, TQ), lambda h: (h, 0, 0, 0)),
        compiler_params=_cparams(("parallel",)),
        name="toeplitz",
    )(vec.reshape(heads, 1, w))


def _t5_bucket(dist):
    exact = N_BUCKETS // 2
    d = jnp.maximum(dist, 0)
    logd = jnp.log(jnp.maximum(d, 1).astype(F32) / exact)
    far = exact + (logd / math.log(BIAS_MAX_DIST / exact) * (N_BUCKETS - exact)).astype(jnp.int32)
    return jnp.where(d < exact, d, jnp.minimum(far, N_BUCKETS - 1))


def _bias_tables(rel_bias, seq):
    per_dist = rel_bias[_t5_bucket(jnp.arange(seq + 1))].T * LOG2E
    n_a = len(DIL_GROUPS) * A_HEADS
    bias_a = []
    for g, (_, dil) in enumerate(DIL_GROUPS):
        t = _toeplitz_tiles(per_dist[g * A_HEADS:(g + 1) * A_HEADS],
                            lambda m, dil=dil: np.where((m >= 0) & (m <= TQ), dil * (TQ - m), -1), TQ, 2)
        if seq // (TQ * dil) > 1:
            with_prev = t.transpose(0, 2, 1, 3).reshape(A_HEADS * TQ, 2 * TQ)
            no_prev = jnp.where(np.arange(2 * TQ)[None, :] < TQ, NEG, with_prev)
            bias_a.append(jnp.stack([no_prev, with_prev]))
        else:
            bias_a.append(t[:, 1].reshape(A_HEADS * TQ, TQ))
    pb = per_dist[n_a:n_a + B_HEADS]
    off_s = (N_SEL_TILES - EXT_STEP // TQ - 1) * TQ
    bias_sel = _toeplitz_tiles(pb, lambda m: off_s - m, TQ, N_SEL_TILES)
    bias_win = _toeplitz_tiles(pb, lambda m: np.where(WIN - m < WIN, WIN - m, -1), TQ, N_WIN_TILES)
    off_c = (N_C_TILES - EXT_STEP // TQ) * TQ
    bias_c = _toeplitz_tiles(per_dist[n_a + B_HEADS:], lambda m: off_c - m, TQC, N_C_TILES)
    return bias_a, bias_sel, bias_win, bias_c


def _pack_w_in(w):
    d = w.shape[0]
    a_w = A_UNITS * LANES
    aq_w = a_w // 3
    bq_w = B_HEADS * HEAD_DIM
    bkv_w = 6 * B_KV_GROUPS * HEAD_DIM
    bg_w = 3 * B_HEADS
    c_w = 3 * C_HEADS * 2 * HEAD_DIM
    cq_w = c_w // 3
    o_bq, o_bkv, o_bg, o_c, o_mg = a_w, a_w + bq_w, a_w + bq_w + bkv_w, a_w + bq_w + bkv_w + bg_w, \
        a_w + bq_w + bkv_w + bg_w + c_w
    colscale = np.ones((w.shape[1],), np.float32)
    for lo, n_q in ((0, aq_w), (o_bq, bq_w), (o_c, cq_w)):
        colscale[lo:lo + n_q] = SCALE * LOG2E
    wb = (w * colscale).astype(BF16)
    a, bq, bkv, bg, cc, mg = (wb[:, :a_w], wb[:, o_bq:o_bkv], wb[:, o_bkv:o_bg], wb[:, o_bg:o_c],
                              wb[:, o_c:o_mg], wb[:, o_mg:])
    bkv = bkv.reshape(d, 6, B_KV_GROUPS, HEAD_DIM)
    order = (2, 2, 4, 4, 3, 3, 5, 5, 0, 1)
    bkv = jnp.stack([bkv[:, o_, g] for g in range(B_KV_GROUPS) for o_ in order], axis=1)
    bkv = bkv.reshape(d, B_KV_GROUPS * BKV_UNITS * LANES)
    bg = bg.reshape(d, B_KV_GROUPS, B_HPG, 3).transpose(0, 1, 3, 2).reshape(d, B_KV_GROUPS, 3 * B_HPG)
    bg = jnp.pad(bg, ((0, 0), (0, 0), (0, LANES - 3 * B_HPG))).reshape(d, B_KV_GROUPS * LANES)
    packed = jnp.concatenate([a, mg, bq, cc, bkv, bg], axis=1)
    assert packed.shape[1] == a_w + NCOL
    return packed


def kernel(x, rel_bias, w_in, cmp_k_pos, cmp_v_pos, cmp_k_w1, cmp_k_w2, cmp_v_w1, cmp_v_w2, diff_lambda, diff_norm, w_branch_a, w_branch_b, w_branch_c, w_out, norm_mix_pre, norm_mix_post, norm_mlp_pre, norm_mlp_post, w_up, w_down):
    bsz, seq, d = x.shape
    n = bsz * seq
    depth = w_in.shape[0]
    nch = seq // CMP_STRIDE
    assert seq % EXT_STEP == 0 and seq // TQ == N_SEL_TILES - EXT_STEP // TQ and (U_MGATE * LANES) % d == 0
    assert N_C_TILES == EXT_STEP // TQ + C_TILES_PER_Q * (seq // TQC - 1)

    bias_a, bias_sel, bias_win, bias_c = _bias_tables(rel_bias, seq)
    et_mat = jnp.asarray(np.arange(seq)[:, None] // SEL_LEN == np.arange(LANES)[None, :], BF16)
    starts = np.arange(nch)[:, None] * CMP_STRIDE
    bstart = np.arange(LANES)[None, :] * SEL_LEN
    ov_mat = jnp.asarray((starts < bstart + SEL_LEN) & (starts + CMP_LEN > bstart), BF16)

    xf = x.reshape(n, d)
    for l in range(depth):
        lam_init = 0.8 - 0.6 * math.exp(-0.3 * l)
        pa, p = _proj(xf, norm_mix_pre[l][None], _pack_w_in(w_in[l]))
        p3 = p.reshape(bsz, seq, NCOL)

        a_out = [_attn_a(pa, bias_a[g], g, dil, bsz, seq) for g, (_, dil) in enumerate(DIL_GROUPS)]
        oa = [o for o, _ in a_out]
        lse = [s for _, s in a_out]

        kvc = jnp.stack([p3[:, :, (U_BKV + g * BKV_UNITS + BKV_UNITS - 1) * LANES:(U_BKV + (g + 1) * BKV_UNITS) * LANES]
                         for g in range(B_KV_GROUPS)], axis=1)
        xc = kvc.reshape(bsz, B_KV_GROUPS, seq, 2, HEAD_DIM).transpose(0, 1, 3, 2, 4)
        xc = xc.reshape(bsz * B_KV_GROUPS, 2, nch, CMP_STRIDE * HEAD_DIM)
        half = CMP_STRIDE * HEAD_DIM
        pos = jnp.stack([cmp_k_pos[l], cmp_v_pos[l]]).reshape(2, 2, half)
        w1 = jnp.stack([cmp_k_w1[l], cmp_v_w1[l]]).reshape(2, 2, half, CMP_HIDDEN).astype(BF16)
        w2 = jnp.stack([cmp_k_w2[l], cmp_v_w2[l]]).astype(BF16)
        kcvc = _cmp(xc, pos, w1, w2)
        ob = _nsa(p3, kcvc, bias_sel, bias_win, et_mat, ov_mat).reshape(n, B_HEADS * HEAD_DIM)

        oc = _diff(p3, bias_c, diff_lambda[l], diff_norm[l][None], lam_init).reshape(n, C_HEADS * 2 * HEAD_DIM)

        xf = _mixout(xf, oa, lse, ob, oc, p,
                     w_branch_a[l].astype(BF16), w_branch_b[l].astype(BF16), w_branch_c[l].astype(BF16),
                     w_out[l].astype(BF16), norm_mix_post[l][None])
        xf = _mlp(xf, norm_mlp_pre[l][None], w_up[l].astype(BF16), w_down[l].astype(BF16),
                  norm_mlp_post[l][None])
    return xf.reshape(bsz, seq, d)
```

```python
import functools
import math

import jax
import jax.numpy as jnp
import numpy as np
from jax import lax
from jax.experimental import pallas as pl
from jax.experimental.pallas import tpu as pltpu

F32 = jnp.float32
BF16 = jnp.bfloat16

LANES = 128
HEAD_DIM = 64
SCALE = HEAD_DIM ** -0.5
LOG2E = math.log2(math.e)
LN2 = math.log(2.0)
NORM_EPS = 1e-6
NEG = -1e30
TINY = 1e-30
TQ = 128
TQC = 512
EXT_STEP = 512

DIL_GROUPS = ((128, 1), (512, 4), (2048, 16))
A_HEADS = 8
A_PAIRS = A_HEADS // 2
B_HEADS = 16
B_KV_GROUPS = 2
B_HPG = B_HEADS // B_KV_GROUPS
C_HEADS = 8
CMP_LEN, CMP_STRIDE, CMP_HIDDEN = 32, 16, 256
SEL_LEN, SEL_TOPK, SEL_LOCAL = 64, 16, 2
FORCED_SCORE = 1e6
WIN = 512
N_BUCKETS = 32
BIAS_MAX_DIST = 2048

A_UNITS = 3 * len(DIL_GROUPS) * A_PAIRS
U_MGATE = 0
U_BQ = 48
U_C = 56
U_BKV = 80
BKV_UNITS = 5
U_BGATE = 90
U_TOTAL = 92
NCOL = U_TOTAL * LANES

VMEM_LIMIT = 56 * 1024 * 1024


def _cparams(sem):
    return pltpu.CompilerParams(dimension_semantics=sem, vmem_limit_bytes=VMEM_LIMIT)


def _rms(x, gain):
    return x * lax.rsqrt(jnp.mean(x * x, axis=-1, keepdims=True) + NORM_EPS) * gain


def _dot_nt(a, b):
    return lax.dot_general(a, b, (((1,), (1,)), ((), ())), preferred_element_type=F32)


def _dot(a, b):
    return jnp.dot(a, b, preferred_element_type=F32)


def _lane_half(shape):
    return lax.broadcasted_iota(jnp.int32, shape, len(shape) - 1) < HEAD_DIM


def _stack_halves(tiles):
    low = _lane_half(tiles[0].shape)
    out = []
    for t in tiles:
        z = jnp.zeros_like(t)
        out += [jnp.where(low, t, z), jnp.where(low, z, t)]
    return jnp.concatenate(out, axis=0)


def _softmax_pv(s, v):
    m = jnp.max(s, axis=-1, keepdims=True)
    p = jnp.exp2(s - m)
    l = jnp.sum(p, axis=-1, keepdims=True)
    return _dot(p.astype(BF16), v) / l


def _proj_kernel(x_ref, g_ref, w_ref, oa_ref, o_ref, h_ref, *, n_a):
    j = pl.program_id(1)

    @pl.when(j == 0)
    def _():
        h_ref[...] = _rms(x_ref[...], g_ref[...]).astype(BF16)

    res = _dot(h_ref[...], w_ref[...])

    @pl.when(j < n_a)
    def _():
        for c in range(oa_ref.shape[0]):
            oa_ref[c] = res[:, c * LANES:(c + 1) * LANES]

    @pl.when(j >= n_a)
    def _():
        o_ref[...] = res.astype(o_ref.dtype)


def _proj(x, gain, w, tm=2048, tn=512):
    n, d = x.shape
    n_a = A_UNITS * LANES // tn
    upb = tn // LANES
    return pl.pallas_call(
        functools.partial(_proj_kernel, n_a=n_a),
        out_shape=(jax.ShapeDtypeStruct((A_UNITS, n, LANES), F32), jax.ShapeDtypeStruct((n, NCOL), BF16)),
        grid=(n // tm, w.shape[1] // tn),
        in_specs=[pl.BlockSpec((tm, d), lambda i, j: (i, 0), pipeline_mode=pl.Buffered(1)),
                  pl.BlockSpec((1, d), lambda i, j: (0, 0)),
                  pl.BlockSpec((d, tn), lambda i, j: (0, j))],
        out_specs=(pl.BlockSpec((upb, tm, LANES), lambda i, j: (jnp.minimum(j, n_a - 1), i, 0)),
                   pl.BlockSpec((tm, tn), lambda i, j: (i, jnp.maximum(j - n_a, 0)))),
        scratch_shapes=[pltpu.VMEM((tm, d), BF16)],
        compiler_params=_cparams(("parallel", "arbitrary")),
        name="proj",
    )(x, gain, w)


def _attn_a_kernel(*refs, dil, use_prev):
    if use_prev:
        q_ref, kc_ref, vc_ref, kp_ref, vp_ref, bias_ref, o_ref, lse_ref = refs
    else:
        q_ref, kc_ref, vc_ref, bias_ref, o_ref, lse_ref = refs
    sb = pl.program_id(1)
    r = pl.program_id(2)
    rows = pl.ds(r, TQ, stride=dil) if dil > 1 else slice(None)
    ld = lambda ref, p: ref[p, rows, :].astype(BF16)
    s = []
    for p in range(A_PAIRS):
        qs = _stack_halves([ld(q_ref, p)])
        ks = jnp.concatenate([ld(kp_ref, p), ld(kc_ref, p)], axis=0) if use_prev else ld(kc_ref, p)
        s.append(_dot_nt(qs, ks))
    bias = bias_ref[(sb > 0).astype(jnp.int32)] if use_prev else bias_ref[...]
    s = jnp.concatenate(s, axis=0) + bias
    m = jnp.max(s, axis=-1, keepdims=True)
    e = jnp.exp2(s - m)
    l = jnp.sum(e, axis=-1, keepdims=True)
    pb = e.astype(BF16)
    lse = (m + jnp.log2(l)) * LN2
    low = _lane_half((TQ, LANES))
    for p in range(A_PAIRS):
        r0 = 2 * p * TQ
        vs = jnp.concatenate([ld(vp_ref, p), ld(vc_ref, p)], axis=0) if use_prev else ld(vc_ref, p)
        o = _dot(pb[r0:r0 + 2 * TQ], vs) / l[r0:r0 + 2 * TQ]
        o_ref[p, rows, :] = jnp.where(low, o[:TQ], o[TQ:])
        lse_ref[p, rows, :] = jnp.where(low, lse[r0:r0 + TQ], lse[r0 + TQ:r0 + 2 * TQ])


def _attn_a(pa, bias, g, dil, bsz, seq):
    n = pa.shape[1]
    blk = TQ * dil
    nsb = seq // blk
    use_prev = nsb > 1
    n_grp = len(DIL_GROUPS)
    cur = lambda qkv: pl.BlockSpec((A_PAIRS, blk, LANES), lambda b, sb, r: (qkv * n_grp + g, b * nsb + sb, 0))
    prev = lambda qkv: pl.BlockSpec((A_PAIRS, blk, LANES),
                                    lambda b, sb, r: (qkv * n_grp + g, b * nsb + jnp.maximum(sb - 1, 0), 0))
    out = pl.BlockSpec((A_PAIRS, blk, LANES), lambda b, sb, r: (0, b * nsb + sb, 0))
    in_specs = [cur(0), cur(1), cur(2)] + ([prev(1), prev(2)] if use_prev else [])
    in_specs.append(pl.BlockSpec(bias.shape, lambda b, sb, r: (0,) * bias.ndim))
    args = [pa] * (len(in_specs) - 1) + [bias]
    return pl.pallas_call(
        functools.partial(_attn_a_kernel, dil=dil, use_prev=use_prev),
        out_shape=(jax.ShapeDtypeStruct((A_PAIRS, n, LANES), F32),) * 2,
        grid=(bsz, nsb, dil),
        in_specs=in_specs,
        out_specs=(out, out),
        compiler_params=_cparams(("parallel", "parallel", "arbitrary")),
        name=f"attn_a_d{dil}",
    )(*args)


def _cmp_kernel(x_ref, pos_ref, w1_ref, w2_ref, o_ref):
    for kv in range(2):
        x = x_ref[kv].astype(F32)
        top = (x + pos_ref[kv, 0:1, :]).astype(BF16)
        bot = (x + pos_ref[kv, 1:2, :]).astype(BF16)
        p1 = _dot(top, w1_ref[kv, 0])
        p2 = _dot(bot, w1_ref[kv, 1])
        hid = p1 + pltpu.roll(p2, p2.shape[0] - 1, 0)
        out = _dot(jax.nn.gelu(hid).astype(BF16), w2_ref[kv]).astype(o_ref.dtype)
        o_ref[kv] = jnp.concatenate([out, out], axis=1)


def _cmp(xc, pos, w1, w2):
    nb = xc.shape[0]
    nch, cw = xc.shape[2], xc.shape[3]
    full = lambda a: pl.BlockSpec(a.shape, lambda i: (0,) * a.ndim)
    return pl.pallas_call(
        _cmp_kernel,
        out_shape=jax.ShapeDtypeStruct((nb, 2, nch, LANES), BF16),
        grid=(nb,),
        in_specs=[pl.BlockSpec((None, 2, nch, cw), lambda i: (i, 0, 0, 0)),
                  full(pos), full(w1), full(w2)],
        out_specs=pl.BlockSpec((None, 2, nch, LANES), lambda i: (i, 0, 0, 0)),
        compiler_params=_cparams(("parallel",)),
        name="cmp",
    )(xc, pos, w1, w2)


N_SEL_TILES = 20
N_WIN_TILES = 9
NSA_CHAIN_HEADS = 4


def _nsa_kernel(q_ref, kv_ref, gate_ref, kcvc_ref, bsel_ref, bwin_ref, et_ref, ov_ref, o_ref):
    i = pl.program_id(2)
    seq = kv_ref.shape[0]
    rows = lax.broadcasted_iota(jnp.int32, (TQ, 1), 0)
    t = i * TQ + rows
    q = q_ref[...]
    qs = _stack_halves([q[:, p * LANES:(p + 1) * LANES] for p in range(B_HPG // 2)])
    gates = jax.nn.sigmoid(gate_ref[...].astype(F32))
    head = lambda a, h: a[h * TQ:(h + 1) * TQ]

    kc = kcvc_ref[0]
    vc = kcvc_ref[1]
    n_cmp = kc.shape[0]
    n_idx = lax.broadcasted_iota(jnp.int32, (1, n_cmp), 1)
    cmask = (n_idx * CMP_STRIDE + (CMP_LEN - 1)) <= t
    c_add = jnp.concatenate([jnp.where(cmask, 0.0, NEG)] * B_HPG, axis=0)
    c_mul = jnp.concatenate([cmask.astype(F32)] * B_HPG, axis=0)
    lc = _dot_nt(qs, kc) + c_add
    e = jnp.exp2(lc - jnp.max(lc, axis=-1, keepdims=True)) * c_mul
    p_cmp = e / jnp.maximum(jnp.sum(e, axis=-1, keepdims=True), TINY)
    o_cmp = _dot(p_cmp.astype(BF16), vc)

    def with_selection_mask():
        psum = head(p_cmp, 0)
        for h in range(1, B_HPG):
            psum = psum + head(p_cmp, h)
        hi = psum.astype(BF16)
        lo = (psum - hi.astype(F32)).astype(BF16)
        score = _dot(hi, ov_ref[...]) + _dot(lo, ov_ref[...])
        jb = lax.broadcasted_iota(jnp.int32, (1, LANES), 1)
        back = lax.shift_right_arithmetic(t, int(math.log2(SEL_LEN))) - jb
        forced = (jb == 0) | ((back >= 0) & (back < SEL_LOCAL))
        score = jnp.where(forced, FORCED_SCORE, score)
        score = jnp.where(jb * SEL_LEN <= t, score, NEG)
        rank = jnp.zeros((TQ, LANES), jnp.int32)
        for ii in range(seq // SEL_LEN):
            col = score[:, ii:ii + 1]
            beats = (col > score) | ((col == score) & (ii < jb))
            rank = rank + beats.astype(jnp.int32)
        sel_neg = jnp.where(rank < SEL_TOPK, 0.0, NEG).astype(BF16)
        return jnp.concatenate([qs, jnp.concatenate([sel_neg] * B_HPG, axis=0)], axis=1)

    n_wt = WIN // TQ + 1
    kt0 = jnp.maximum(i - WIN // TQ, 0)
    k0 = pl.multiple_of(kt0 * TQ, TQ)
    w0 = WIN // TQ - i + kt0
    k_w = kv_ref[pl.ds(k0, n_wt * TQ), LANES:2 * LANES]
    v_w = kv_ref[pl.ds(k0, n_wt * TQ), 3 * LANES:4 * LANES]

    def chains(q_all, k, v, bias_ref, first_tile, n_tiles):
        outs = []
        s_all = _dot_nt(q_all, k)
        for h0 in range(0, B_HPG, NSA_CHAIN_HEADS):
            bias = jnp.concatenate(
                [jnp.concatenate([bias_ref[h, first_tile + kk] for kk in range(n_tiles)], axis=1)
                 for h in range(h0, h0 + NSA_CHAIN_HEADS)], axis=0)
            outs.append(_softmax_pv(s_all[h0 * TQ:(h0 + NSA_CHAIN_HEADS) * TQ] + bias, v))
        return jnp.concatenate(outs, axis=0)

    o_win = chains(qs, k_w, v_w, bwin_ref, w0, n_wt)

    def finish(o_sel):
        low = _lane_half((TQ, LANES))
        outs = []
        for h in range(B_HPG):
            outs.append(gates[:, h:h + 1] * head(o_cmp, h)
                        + gates[:, B_HPG + h:B_HPG + h + 1] * head(o_sel, h)
                        + gates[:, 2 * B_HPG + h:2 * B_HPG + h + 1] * head(o_win, h))
        for p in range(B_HPG // 2):
            o_ref[:, p * LANES:(p + 1) * LANES] = jnp.where(low, outs[2 * p], outs[2 * p + 1]).astype(o_ref.dtype)

    tiles_per_ext = EXT_STEP // TQ
    for v in range(seq // EXT_STEP):
        @pl.when(i // tiles_per_ext == v)
        def _(v=v):
            ext = (v + 1) * EXT_STEP
            b0 = (N_SEL_TILES - tiles_per_ext - 1) - i
            if ext <= SEL_TOPK * SEL_LEN:
                q_all, ks = qs, kv_ref[0:ext, 0:LANES]
            else:
                q_all = with_selection_mask()
                ks = jnp.concatenate([kv_ref[0:ext, 0:LANES], et_ref[0:ext, :]], axis=1)
            finish(chains(q_all, ks, kv_ref[0:ext, 2 * LANES:3 * LANES], bsel_ref, b0, ext // TQ))


def _nsa(p3, kcvc, bsel, bwin, et_mat, ov_mat):
    b, s, _ = p3.shape
    nq = s // TQ
    qw = B_HPG * HEAD_DIM
    kvw = BKV_UNITS * LANES
    full = lambda a: pl.BlockSpec(a.shape, lambda g, bb, i: (0,) * a.ndim)
    per_group = lambda a: pl.BlockSpec((B_HPG,) + a.shape[1:], lambda g, bb, i: (g, 0, 0, 0),
                                       pipeline_mode=pl.Buffered(1))
    return pl.pallas_call(
        _nsa_kernel,
        out_shape=jax.ShapeDtypeStruct((b, s, B_HEADS * HEAD_DIM), BF16),
        grid=(B_KV_GROUPS, b, nq),
        in_specs=[
            pl.BlockSpec((None, TQ, qw), lambda g, bb, i: (bb, i, U_BQ * LANES // qw + g)),
            pl.BlockSpec((None, s, kvw), lambda g, bb, i: (bb, 0, U_BKV * LANES // kvw + g)),
            pl.BlockSpec((None, TQ, LANES), lambda g, bb, i: (bb, i, U_BGATE + g)),
            pl.BlockSpec((None, 2, kcvc.shape[2], LANES), lambda g, bb, i: (bb * B_KV_GROUPS + g, 0, 0, 0)),
            per_group(bsel), per_group(bwin), full(et_mat), full(ov_mat)],
        out_specs=pl.BlockSpec((None, TQ, qw), lambda g, bb, i: (bb, i, g)),
        compiler_params=_cparams(("parallel", "parallel", "arbitrary")),
        name="nsa",
    )(p3, p3, p3, kcvc, bsel, bwin, et_mat, ov_mat)


C_TILES_PER_Q = TQC // TQ
DIFF_CHAIN_ROWS = 256
N_C_TILES = 16


def _diff_kernel(q_ref, k_ref, v_ref, bias_ref, lam_ref, gain_ref, o_ref, *, lam_init):
    i = pl.program_id(2)
    seq = k_ref.shape[0]
    lv = lam_ref[...]
    lam = (jnp.exp(jnp.sum(lv[0:1] * lv[1:2], axis=-1, keepdims=True))
           - jnp.exp(jnp.sum(lv[2:3] * lv[3:4], axis=-1, keepdims=True)) + lam_init)
    qs = _stack_halves([q_ref[...]])
    q_per_ext = EXT_STEP // TQC
    for v in range(seq // EXT_STEP):
        @pl.when(i // q_per_ext == v)
        def _(v=v):
            ext = (v + 1) * EXT_STEP
            b0 = (N_C_TILES - EXT_STEP // TQ) - C_TILES_PER_Q * i
            bias = jnp.concatenate([bias_ref[b0 + kk] for kk in range(ext // TQ)], axis=1)
            bias2 = jnp.concatenate([bias, bias], axis=0)
            o = jnp.concatenate(
                [_softmax_pv(_dot_nt(qs[r0:r0 + DIFF_CHAIN_ROWS], k_ref[0:ext, :]) + bias2[r0:r0 + DIFF_CHAIN_ROWS],
                             v_ref[0:ext, :]) for r0 in range(0, 2 * TQC, DIFF_CHAIN_ROWS)], axis=0)
            o = o[:TQC] - lam * o[TQC:]
            o_ref[...] = (_rms(o, gain_ref[...]) * (1.0 - lam_init)).astype(o_ref.dtype)


def _diff(p3, bias, lam_vecs, sub_gain, lam_init):
    b, s, _ = p3.shape
    dv = 2 * HEAD_DIM
    full = lambda a: pl.BlockSpec(a.shape, lambda h, bb, i: (0,) * a.ndim)
    return pl.pallas_call(
        functools.partial(_diff_kernel, lam_init=lam_init),
        out_shape=jax.ShapeDtypeStruct((b, s, C_HEADS * dv), BF16),
        grid=(C_HEADS, b, s // TQC),
        in_specs=[
            pl.BlockSpec((None, TQC, dv), lambda h, bb, i: (bb, i, U_C + h)),
            pl.BlockSpec((None, s, dv), lambda h, bb, i: (bb, 0, U_C + C_HEADS + h)),
            pl.BlockSpec((None, s, dv), lambda h, bb, i: (bb, 0, U_C + 2 * C_HEADS + h)),
            pl.BlockSpec((None,) + bias.shape[1:], lambda h, bb, i: (h, 0, 0, 0)),
            full(lam_vecs), full(sub_gain)],
        out_specs=pl.BlockSpec((None, TQC, dv), lambda h, bb, i: (bb, i, h)),
        compiler_params=_cparams(("parallel", "parallel", "arbitrary")),
        name="diff",
    )(p3, p3, p3, bias, lam_vecs, sub_gain)


def _mixout_kernel(x_ref, oa0_ref, oa1_ref, oa2_ref, l0_ref, l1_ref, l2_ref, ob_ref, oc_ref,
                   g0_ref, g1_ref, g2_ref, wa_ref, wb_ref, wc_ref, wo_ref, gain_ref, o_ref):
    planes = lambda ref: jnp.concatenate([ref[p] for p in range(ref.shape[0])], axis=1)
    lse = [planes(l0_ref), planes(l1_ref), planes(l2_ref)]
    m = jnp.maximum(jnp.maximum(lse[0], lse[1]), lse[2])
    e = [jnp.exp(v - m) for v in lse]
    den = e[0] + e[1] + e[2]
    oa = (e[0] / den) * planes(oa0_ref) + (e[1] / den) * planes(oa1_ref) + (e[2] / den) * planes(oa2_ref)
    mixed = (jax.nn.sigmoid(g0_ref[...].astype(F32)) * _dot(oa.astype(BF16), wa_ref[...])
             + jax.nn.sigmoid(g1_ref[...].astype(F32)) * _dot(ob_ref[...], wb_ref[...])
             + jax.nn.sigmoid(g2_ref[...].astype(F32)) * _dot(oc_ref[...], wc_ref[...]))
    y = _dot(mixed.astype(BF16), wo_ref[...])
    o_ref[...] = x_ref[...] + _rms(y, gain_ref[...])


def _mixout(x, oa, lse, ob, oc, p, wa, wb, wc, wo, gain, tm=256):
    n, d = x.shape
    const = lambda a: pl.BlockSpec(a.shape, lambda i: (0,) * a.ndim, pipeline_mode=pl.Buffered(1))
    row = lambda a: pl.BlockSpec((tm, a.shape[1]), lambda i: (i, 0))
    grp = lambda a: pl.BlockSpec((a.shape[0], tm, a.shape[2]), lambda i: (0, i, 0))
    gate = lambda br: pl.BlockSpec((tm, d), lambda i: (i, U_MGATE * LANES // d + br))
    return pl.pallas_call(
        _mixout_kernel,
        out_shape=jax.ShapeDtypeStruct((n, d), F32),
        grid=(n // tm,),
        in_specs=[row(x)] + [grp(a) for a in oa] + [grp(a) for a in lse] + [row(ob), row(oc),
                  gate(0), gate(1), gate(2), const(wa), const(wb), const(wc), const(wo), const(gain)],
        out_specs=row(x),
        compiler_params=_cparams(("parallel",)),
        name="mixout",
    )(x, *oa, *lse, ob, oc, p, p, p, wa, wb, wc, wo, gain)


def _mlp_kernel(x_ref, gpre_ref, wup_ref, wdn_ref, gpost_ref, o_ref, h_ref, acc_ref):
    f = pl.program_id(1)

    @pl.when(f == 0)
    def _():
        h_ref[...] = _rms(x_ref[...], gpre_ref[...]).astype(BF16)
        acc_ref[...] = jnp.zeros_like(acc_ref)

    u = jnp.square(jax.nn.relu(_dot(h_ref[...], wup_ref[...])))
    acc_ref[...] += _dot(u.astype(BF16), wdn_ref[...])

    @pl.when(f == pl.num_programs(1) - 1)
    def _():
        o_ref[...] = x_ref[...] + _rms(acc_ref[...], gpost_ref[...])


def _mlp(x, gpre, wup, wdn, gpost, tm=512, tf=1024):
    n, d = x.shape
    dff = wup.shape[1]
    return pl.pallas_call(
        _mlp_kernel,
        out_shape=jax.ShapeDtypeStruct((n, d), F32),
        grid=(n // tm, dff // tf),
        in_specs=[pl.BlockSpec((tm, d), lambda i, f: (i, 0)),
                  pl.BlockSpec((1, d), lambda i, f: (0, 0)),
                  pl.BlockSpec((d, tf), lambda i, f: (0, f)),
                  pl.BlockSpec((tf, d), lambda i, f: (f, 0)),
                  pl.BlockSpec((1, d), lambda i, f: (0, 0))],
        out_specs=pl.BlockSpec((tm, d), lambda i, f: (i, 0)),
        scratch_shapes=[pltpu.VMEM((tm, d), BF16), pltpu.VMEM((tm, d), F32)],
        compiler_params=_cparams(("parallel", "arbitrary")),
        name="mlp",
    )(x, gpre, wup, wdn, gpost)


def _toeplitz_kernel(v_ref, o_ref):
    n_tiles, rows, _ = o_ref.shape
    x = jnp.broadcast_to(v_ref[...], (rows, v_ref.shape[-1]))
    t = pltpu.roll(x, 0, 1, stride=1, stride_axis=0)
    for k in range(n_tiles):
        o_ref[k] = t[:, k * TQ:(k + 1) * TQ]


def _toeplitz_tiles(per_dist, dist_of_m, rows, n_tiles):
    width = n_tiles * TQ
    w = width + rows
    m = np.arange(w)
    m = np.where(m >= width, m - w, m)
    dist = dist_of_m(m)
    neg_col = per_dist.shape[1]
    idx = np.where(dist >= 0, np.minimum(dist, neg_col - 1), neg_col)
    heads = per_dist.shape[0]
    vec = jnp.concatenate([per_dist, jnp.full((heads, 1), NEG, F32)], axis=1)[:, idx]
    return pl.pallas_call(
        _toeplitz_kernel,
        out_shape=jax.ShapeDtypeStruct((heads, n_tiles, rows, TQ), F32),
        grid=(heads,),
        in_specs=[pl.BlockSpec((None, 1, w), lambda h: (h, 0, 0))],
        out_specs=pl.BlockSpec((None, n_tiles, rows, TQ), lambda h: (h, 0, 0, 0)),
        compiler_params=_cparams(("parallel",)),
        name="toeplitz",
    )(vec.reshape(heads, 1, w))


def _t5_bucket(dist):
    exact = N_BUCKETS // 2
    d = jnp.maximum(dist, 0)
    logd = jnp.log(jnp.maximum(d, 1).astype(F32) / exact)
    far = exact + (logd / math.log(BIAS_MAX_DIST / exact) * (N_BUCKETS - exact)).astype(jnp.int32)
    return jnp.where(d < exact, d, jnp.minimum(far, N_BUCKETS - 1))


def _bias_tables(rel_bias, seq):
    per_dist = rel_bias[_t5_bucket(jnp.arange(seq + 1))].T * LOG2E
    n_a = len(DIL_GROUPS) * A_HEADS
    bias_a = []
    for g, (_, dil) in enumerate(DIL_GROUPS):
        t = _toeplitz_tiles(per_dist[g * A_HEADS:(g + 1) * A_HEADS],
                            lambda m, dil=dil: np.where((m >= 0) & (m <= TQ), dil * (TQ - m), -1), TQ, 2)
        if seq // (TQ * dil) > 1:
            with_prev = t.transpose(0, 2, 1, 3).reshape(A_HEADS * TQ, 2 * TQ)
            no_prev = jnp.where(np.arange(2 * TQ)[None, :] < TQ, NEG, with_prev)
            bias_a.append(jnp.stack([no_prev, with_prev]))
        else:
            bias_a.append(t[:, 1].reshape(A_HEADS * TQ, TQ))
    pb = per_dist[n_a:n_a + B_HEADS]
    off_s = (N_SEL_TILES - EXT_STEP // TQ - 1) * TQ
    bias_sel = _toeplitz_tiles(pb, lambda m: off_s - m, TQ, N_SEL_TILES)
    bias_win = _toeplitz_tiles(pb, lambda m: np.where(WIN - m < WIN, WIN - m, -1), TQ, N_WIN_TILES)
    off_c = (N_C_TILES - EXT_STEP // TQ) * TQ
    bias_c = _toeplitz_tiles(per_dist[n_a + B_HEADS:], lambda m: off_c - m, TQC, N_C_TILES)
    return bias_a, bias_sel, bias_win, bias_c


def _pack_w_in(w):
    d = w.shape[0]
    a_w = A_UNITS * LANES
    aq_w = a_w // 3
    bq_w = B_HEADS * HEAD_DIM
    bkv_w = 6 * B_KV_GROUPS * HEAD_DIM
    bg_w = 3 * B_HEADS
    c_w = 3 * C_HEADS * 2 * HEAD_DIM
    cq_w = c_w // 3
    o_bq, o_bkv, o_bg, o_c, o_mg = a_w, a_w + bq_w, a_w + bq_w + bkv_w, a_w + bq_w + bkv_w + bg_w, \
        a_w + bq_w + bkv_w + bg_w + c_w
    colscale = np.ones((w.shape[1],), np.float32)
    for lo, n_q in ((0, aq_w), (o_bq, bq_w), (o_c, cq_w)):
        colscale[lo:lo + n_q] = SCALE * LOG2E
    wb = (w * colscale).astype(BF16)
    a, bq, bkv, bg, cc, mg = (wb[:, :a_w], wb[:, o_bq:o_bkv], wb[:, o_bkv:o_bg], wb[:, o_bg:o_c],
                              wb[:, o_c:o_mg], wb[:, o_mg:])
    bkv = bkv.reshape(d, 6, B_KV_GROUPS, HEAD_DIM)
    order = (2, 2, 4, 4, 3, 3, 5, 5, 0, 1)
    bkv = jnp.stack([bkv[:, o_, g] for g in range(B_KV_GROUPS) for o_ in order], axis=1)
    bkv = bkv.reshape(d, B_KV_GROUPS * BKV_UNITS * LANES)
    bg = bg.reshape(d, B_KV_GROUPS, B_HPG, 3).transpose(0, 1, 3, 2).reshape(d, B_KV_GROUPS, 3 * B_HPG)
    bg = jnp.pad(bg, ((0, 0), (0, 0), (0, LANES - 3 * B_HPG))).reshape(d, B_KV_GROUPS * LANES)
    packed = jnp.concatenate([a, mg, bq, cc, bkv, bg], axis=1)
    assert packed.shape[1] == a_w + NCOL
    return packed


def kernel(x, rel_bias, w_in, cmp_k_pos, cmp_v_pos, cmp_k_w1, cmp_k_w2, cmp_v_w1, cmp_v_w2, diff_lambda, diff_norm, w_branch_a, w_branch_b, w_branch_c, w_out, norm_mix_pre, norm_mix_post, norm_mlp_pre, norm_mlp_post, w_up, w_down):
    bsz, seq, d = x.shape
    n = bsz * seq
    depth = w_in.shape[0]
    nch = seq // CMP_STRIDE
    assert seq % EXT_STEP == 0 and seq // TQ == N_SEL_TILES - EXT_STEP // TQ and (U_MGATE * LANES) % d == 0
    assert N_C_TILES == EXT_STEP // TQ + C_TILES_PER_Q * (seq // TQC - 1)

    bias_a, bias_sel, bias_win, bias_c = _bias_tables(rel_bias, seq)
    et_mat = jnp.asarray(np.arange(seq)[:, None] // SEL_LEN == np.arange(LANES)[None, :], BF16)
    starts = np.arange(nch)[:, None] * CMP_STRIDE
    bstart = np.arange(LANES)[None, :] * SEL_LEN
    ov_mat = jnp.asarray((starts < bstart + SEL_LEN) & (starts + CMP_LEN > bstart), BF16)

    xf = x.reshape(n, d)
    for l in range(depth):
        lam_init = 0.8 - 0.6 * math.exp(-0.3 * l)
        pa, p = _proj(xf, norm_mix_pre[l][None], _pack_w_in(w_in[l]))
        p3 = p.reshape(bsz, seq, NCOL)

        a_out = [_attn_a(pa, bias_a[g], g, dil, bsz, seq) for g, (_, dil) in enumerate(DIL_GROUPS)]
        oa = [o for o, _ in a_out]
        lse = [s for _, s in a_out]

        kvc = jnp.stack([p3[:, :, (U_BKV + g * BKV_UNITS + BKV_UNITS - 1) * LANES:(U_BKV + (g + 1) * BKV_UNITS) * LANES]
                         for g in range(B_KV_GROUPS)], axis=1)
        xc = kvc.reshape(bsz, B_KV_GROUPS, seq, 2, HEAD_DIM).transpose(0, 1, 3, 2, 4)
        xc = xc.reshape(bsz * B_KV_GROUPS, 2, nch, CMP_STRIDE * HEAD_DIM)
        half = CMP_STRIDE * HEAD_DIM
        pos = jnp.stack([cmp_k_pos[l], cmp_v_pos[l]]).reshape(2, 2, half)
        w1 = jnp.stack([cmp_k_w1[l], cmp_v_w1[l]]).reshape(2, 2, half, CMP_HIDDEN).astype(BF16)
        w2 = jnp.stack([cmp_k_w2[l], cmp_v_w2[l]]).astype(BF16)
        kcvc = _cmp(xc, pos, w1, w2)
        ob = _nsa(p3, kcvc, bias_sel, bias_win, et_mat, ov_mat).reshape(n, B_HEADS * HEAD_DIM)

        oc = _diff(p3, bias_c, diff_lambda[l], diff_norm[l][None], lam_init).reshape(n, C_HEADS * 2 * HEAD_DIM)

        xf = _mixout(xf, oa, lse, ob, oc, p,
                     w_branch_a[l].astype(BF16), w_branch_b[l].astype(BF16), w_branch_c[l].astype(BF16),
                     w_out[l].astype(BF16), norm_mix_post[l][None])
        xf = _mlp(xf, norm_mlp_pre[l][None], w_up[l].astype(BF16), w_down[l].astype(BF16),
                  norm_mlp_post[l][None])
    return xf.reshape(bsz, seq, d)
```

```python
import functools
import math

import jax
import jax.numpy as jnp
import numpy as np
from jax import lax
from jax.experimental import pallas as pl
from jax.experimental.pallas import tpu as pltpu

F32 = jnp.float32
BF16 = jnp.bfloat16

LANES = 128
HEAD_DIM = 64
SCALE = HEAD_DIM ** -0.5
LOG2E = math.log2(math.e)
LN2 = math.log(2.0)
NORM_EPS = 1e-6
NEG = -1e30
TINY = 1e-30
TQ = 128
TQC = 512
EXT_STEP = 512

DIL_GROUPS = ((128, 1), (512, 4), (2048, 16))
A_HEADS = 8
A_PAIRS = A_HEADS // 2
B_HEADS = 16
B_KV_GROUPS = 2
B_HPG = B_HEADS // B_KV_GROUPS
C_HEADS = 8
CMP_LEN, CMP_STRIDE, CMP_HIDDEN = 32, 16, 256
SEL_LEN, SEL_TOPK, SEL_LOCAL = 64, 16, 2
FORCED_SCORE = 1e6
WIN = 512
N_BUCKETS = 32
BIAS_MAX_DIST = 2048

A_UNITS = 3 * len(DIL_GROUPS) * A_PAIRS
U_MGATE = 0
U_BQ = 48
U_C = 56
U_BKV = 80
BKV_UNITS = 5
U_BGATE = 90
U_TOTAL = 92
NCOL = U_TOTAL * LANES

VMEM_LIMIT = 56 * 1024 * 1024


def _cparams(sem):
    return pltpu.CompilerParams(dimension_semantics=sem, vmem_limit_bytes=VMEM_LIMIT)


def _rms(x, gain):
    return x * lax.rsqrt(jnp.mean(x * x, axis=-1, keepdims=True) + NORM_EPS) * gain


def _dot_nt(a, b):
    return lax.dot_general(a, b, (((1,), (1,)), ((), ())), preferred_element_type=F32)


def _dot(a, b):
    return jnp.dot(a, b, preferred_element_type=F32)


def _lane_half(shape):
    return lax.broadcasted_iota(jnp.int32, shape, len(shape) - 1) < HEAD_DIM


def _stack_halves(tiles):
    low = _lane_half(tiles[0].shape)
    out = []
    for t in tiles:
        z = jnp.zeros_like(t)
        out += [jnp.where(low, t, z), jnp.where(low, z, t)]
    return jnp.concatenate(out, axis=0)


def _softmax_pv(s, v):
    m = jnp.max(s, axis=-1, keepdims=True)
    p = jnp.exp2(s - m)
    l = jnp.sum(p, axis=-1, keepdims=True)
    return _dot(p.astype(BF16), v) / l


def _proj_kernel(x_ref, g_ref, w_ref, oa_ref, o_ref, h_ref, *, n_a):
    j = pl.program_id(1)

    @pl.when(j == 0)
    def _():
        h_ref[...] = _rms(x_ref[...], g_ref[...]).astype(BF16)

    res = _dot(h_ref[...], w_ref[...])

    @pl.when(j < n_a)
    def _():
        for c in range(oa_ref.shape[0]):
            oa_ref[c] = res[:, c * LANES:(c + 1) * LANES]

    @pl.when(j >= n_a)
    def _():
        o_ref[...] = res.astype(o_ref.dtype)


def _proj(x, gain, w, tm=2048, tn=512):
    n, d = x.shape
    n_a = A_UNITS * LANES // tn
    upb = tn // LANES
    return pl.pallas_call(
        functools.partial(_proj_kernel, n_a=n_a),
        out_shape=(jax.ShapeDtypeStruct((A_UNITS, n, LANES), F32), jax.ShapeDtypeStruct((n, NCOL), BF16)),
        grid=(n // tm, w.shape[1] // tn),
        in_specs=[pl.BlockSpec((tm, d), lambda i, j: (i, 0), pipeline_mode=pl.Buffered(1)),
                  pl.BlockSpec((1, d), lambda i, j: (0, 0)),
                  pl.BlockSpec((d, tn), lambda i, j: (0, j))],
        out_specs=(pl.BlockSpec((upb, tm, LANES), lambda i, j: (jnp.minimum(j, n_a - 1), i, 0)),
                   pl.BlockSpec((tm, tn), lambda i, j: (i, jnp.maximum(j - n_a, 0)))),
        scratch_shapes=[pltpu.VMEM((tm, d), BF16)],
        compiler_params=_cparams(("parallel", "arbitrary")),
        name="proj",
    )(x, gain, w)


def _attn_a_kernel(*refs, dil, use_prev):
    if use_prev:
        q_ref, kc_ref, vc_ref, kp_ref, vp_ref, bias_ref, o_ref, lse_ref = refs
    else:
        q_ref, kc_ref, vc_ref, bias_ref, o_ref, lse_ref = refs
    sb = pl.program_id(1)
    r = pl.program_id(2)
    rows = pl.ds(r, TQ, stride=dil) if dil > 1 else slice(None)
    ld = lambda ref, p: ref[p, rows, :].astype(BF16)
    s = []
    for p in range(A_PAIRS):
        qs = _stack_halves([ld(q_ref, p)])
        ks = jnp.concatenate([ld(kp_ref, p), ld(kc_ref, p)], axis=0) if use_prev else ld(kc_ref, p)
        s.append(_dot_nt(qs, ks))
    bias = bias_ref[(sb > 0).astype(jnp.int32)] if use_prev else bias_ref[...]
    s = jnp.concatenate(s, axis=0) + bias
    m = jnp.max(s, axis=-1, keepdims=True)
    e = jnp.exp2(s - m)
    l = jnp.sum(e, axis=-1, keepdims=True)
    pb = e.astype(BF16)
    lse = (m + jnp.log2(l)) * LN2
    low = _lane_half((TQ, LANES))
    for p in range(A_PAIRS):
        r0 = 2 * p * TQ
        vs = jnp.concatenate([ld(vp_ref, p), ld(vc_ref, p)], axis=0) if use_prev else ld(vc_ref, p)
        o = _dot(pb[r0:r0 + 2 * TQ], vs) / l[r0:r0 + 2 * TQ]
        o_ref[p, rows, :] = jnp.where(low, o[:TQ], o[TQ:])
        lse_ref[p, rows, :] = jnp.where(low, lse[r0:r0 + TQ], lse[r0 + TQ:r0 + 2 * TQ])


def _attn_a(pa, bias, g, dil, bsz, seq):
    n = pa.shape[1]
    blk = TQ * dil
    nsb = seq // blk
    use_prev = nsb > 1
    n_grp = len(DIL_GROUPS)
    cur = lambda qkv: pl.BlockSpec((A_PAIRS, blk, LANES), lambda b, sb, r: (qkv * n_grp + g, b * nsb + sb, 0))
    prev = lambda qkv: pl.BlockSpec((A_PAIRS, blk, LANES),
                                    lambda b, sb, r: (qkv * n_grp + g, b * nsb + jnp.maximum(sb - 1, 0), 0))
    out = pl.BlockSpec((A_PAIRS, blk, LANES), lambda b, sb, r: (0, b * nsb + sb, 0))
    in_specs = [cur(0), cur(1), cur(2)] + ([prev(1), prev(2)] if use_prev else [])
    in_specs.append(pl.BlockSpec(bias.shape, lambda b, sb, r: (0,) * bias.ndim))
    args = [pa] * (len(in_specs) - 1) + [bias]
    return pl.pallas_call(
        functools.partial(_attn_a_kernel, dil=dil, use_prev=use_prev),
        out_shape=(jax.ShapeDtypeStruct((A_PAIRS, n, LANES), F32),) * 2,
        grid=(bsz, nsb, dil),
        in_specs=in_specs,
        out_specs=(out, out),
        compiler_params=_cparams(("parallel", "parallel", "arbitrary")),
        name=f"attn_a_d{dil}",
    )(*args)


def _cmp_kernel(x_ref, pos_ref, w1_ref, w2_ref, o_ref):
    for kv in range(2):
        x = x_ref[kv].astype(F32)
        top = (x + pos_ref[kv, 0:1, :]).astype(BF16)
        bot = (x + pos_ref[kv, 1:2, :]).astype(BF16)
        p1 = _dot(top, w1_ref[kv, 0])
        p2 = _dot(bot, w1_ref[kv, 1])
        hid = p1 + pltpu.roll(p2, p2.shape[0] - 1, 0)
        out = _dot(jax.nn.gelu(hid).astype(BF16), w2_ref[kv]).astype(o_ref.dtype)
        o_ref[kv] = jnp.concatenate([out, out], axis=1)


def _cmp(xc, pos, w1, w2):
    nb = xc.shape[0]
    nch, cw = xc.shape[2], xc.shape[3]
    full = lambda a: pl.BlockSpec(a.shape, lambda i: (0,) * a.ndim)
    return pl.pallas_call(
        _cmp_kernel,
        out_shape=jax.ShapeDtypeStruct((nb, 2, nch, LANES), BF16),
        grid=(nb,),
        in_specs=[pl.BlockSpec((None, 2, nch, cw), lambda i: (i, 0, 0, 0)),
                  full(pos), full(w1), full(w2)],
        out_specs=pl.BlockSpec((None, 2, nch, LANES), lambda i: (i, 0, 0, 0)),
        compiler_params=_cparams(("parallel",)),
        name="cmp",
    )(xc, pos, w1, w2)


N_SEL_TILES = 20
N_WIN_TILES = 9
NSA_CHAIN_HEADS = 4


def _nsa_kernel(q_ref, kv_ref, gate_ref, kcvc_ref, bsel_ref, bwin_ref, et_ref, ov_ref, o_ref):
    i = pl.program_id(2)
    seq = kv_ref.shape[0]
    rows = lax.broadcasted_iota(jnp.int32, (TQ, 1), 0)
    t = i * TQ + rows
    q = q_ref[...]
    qs = _stack_halves([q[:, p * LANES:(p + 1) * LANES] for p in range(B_HPG // 2)])
    gates = jax.nn.sigmoid(gate_ref[...].astype(F32))
    head = lambda a, h: a[h * TQ:(h + 1) * TQ]

    kc = kcvc_ref[0]
    vc = kcvc_ref[1]
    n_cmp = kc.shape[0]
    n_idx = lax.broadcasted_iota(jnp.int32, (1, n_cmp), 1)
    cmask = (n_idx * CMP_STRIDE + (CMP_LEN - 1)) <= t
    c_add = jnp.concatenate([jnp.where(cmask, 0.0, NEG)] * B_HPG, axis=0)
    c_mul = jnp.concatenate([cmask.astype(F32)] * B_HPG, axis=0)
    lc = _dot_nt(qs, kc) + c_add
    e = jnp.exp2(lc - jnp.max(lc, axis=-1, keepdims=True)) * c_mul
    p_cmp = e / jnp.maximum(jnp.sum(e, axis=-1, keepdims=True), TINY)
    o_cmp = _dot(p_cmp.astype(BF16), vc)

    def with_selection_mask():
        psum = head(p_cmp, 0)
        for h in range(1, B_HPG):
            psum = psum + head(p_cmp, h)
        hi = psum.astype(BF16)
        lo = (psum - hi.astype(F32)).astype(BF16)
        score = _dot(hi, ov_ref[...]) + _dot(lo, ov_ref[...])
        jb = lax.broadcasted_iota(jnp.int32, (1, LANES), 1)
        back = lax.shift_right_arithmetic(t, int(math.log2(SEL_LEN))) - jb
        forced = (jb == 0) | ((back >= 0) & (back < SEL_LOCAL))
        score = jnp.where(forced, FORCED_SCORE, score)
        score = jnp.where(jb * SEL_LEN <= t, score, NEG)
        rank = jnp.zeros((TQ, LANES), jnp.int32)
        for ii in range(seq // SEL_LEN):
            col = score[:, ii:ii + 1]
            beats = (col > score) | ((col == score) & (ii < jb))
            rank = rank + beats.astype(jnp.int32)
        sel_neg = jnp.where(rank < SEL_TOPK, 0.0, NEG).astype(BF16)
        return jnp.concatenate([qs, jnp.concatenate([sel_neg] * B_HPG, axis=0)], axis=1)

    n_wt = WIN // TQ + 1
    kt0 = jnp.maximum(i - WIN // TQ, 0)
    k0 = pl.multiple_of(kt0 * TQ, TQ)
    w0 = WIN // TQ - i + kt0
    k_w = kv_ref[pl.ds(k0, n_wt * TQ), LANES:2 * LANES]
    v_w = kv_ref[pl.ds(k0, n_wt * TQ), 3 * LANES:4 * LANES]

    def chains(q_all, k, v, bias_ref, first_tile, n_tiles):
        outs = []
        s_all = _dot_nt(q_all, k)
        for h0 in range(0, B_HPG, NSA_CHAIN_HEADS):
            bias = jnp.concatenate(
                [jnp.concatenate([bias_ref[h, first_tile + kk] for kk in range(n_tiles)], axis=1)
                 for h in range(h0, h0 + NSA_CHAIN_HEADS)], axis=0)
            outs.append(_softmax_pv(s_all[h0 * TQ:(h0 + NSA_CHAIN_HEADS) * TQ] + bias, v))
        return jnp.concatenate(outs, axis=0)

    o_win = chains(qs, k_w, v_w, bwin_ref, w0, n_wt)

    def finish(o_sel):
        low = _lane_half((TQ, LANES))
        outs = []
        for h in range(B_HPG):
            outs.append(gates[:, h:h + 1] * head(o_cmp, h)
                        + gates[:, B_HPG + h:B_HPG + h + 1] * head(o_sel, h)
                        + gates[:, 2 * B_HPG + h:2 * B_HPG + h + 1] * head(o_win, h))
        for p in range(B_HPG // 2):
            o_ref[:, p * LANES:(p + 1) * LANES] = jnp.where(low, outs[2 * p], outs[2 * p + 1]).astype(o_ref.dtype)

    tiles_per_ext = EXT_STEP // TQ
    for v in range(seq // EXT_STEP):
        @pl.when(i // tiles_per_ext == v)
        def _(v=v):
            ext = (v + 1) * EXT_STEP
            b0 = (N_SEL_TILES - tiles_per_ext - 1) - i
            if ext <= SEL_TOPK * SEL_LEN:
                q_all, ks = qs, kv_ref[0:ext, 0:LANES]
            else:
                q_all = with_selection_mask()
                ks = jnp.concatenate([kv_ref[0:ext, 0:LANES], et_ref[0:ext, :]], axis=1)
            finish(chains(q_all, ks, kv_ref[0:ext, 2 * LANES:3 * LANES], bsel_ref, b0, ext // TQ))


def _nsa(p3, kcvc, bsel, bwin, et_mat, ov_mat):
    b, s, _ = p3.shape
    nq = s // TQ
    qw = B_HPG * HEAD_DIM
    kvw = BKV_UNITS * LANES
    full = lambda a: pl.BlockSpec(a.shape, lambda g, bb, i: (0,) * a.ndim)
    per_group = lambda a: pl.BlockSpec((B_HPG,) + a.shape[1:], lambda g, bb, i: (g, 0, 0, 0),
                                       pipeline_mode=pl.Buffered(1))
    return pl.pallas_call(
        _nsa_kernel,
        out_shape=jax.ShapeDtypeStruct((b, s, B_HEADS * HEAD_DIM), BF16),
        grid=(B_KV_GROUPS, b, nq),
        in_specs=[
            pl.BlockSpec((None, TQ, qw), lambda g, bb, i: (bb, i, U_BQ * LANES // qw + g)),
            pl.BlockSpec((None, s, kvw), lambda g, bb, i: (bb, 0, U_BKV * LANES // kvw + g)),
            pl.BlockSpec((None, TQ, LANES), lambda g, bb, i: (bb, i, U_BGATE + g)),
            pl.BlockSpec((None, 2, kcvc.shape[2], LANES), lambda g, bb, i: (bb * B_KV_GROUPS + g, 0, 0, 0)),
            per_group(bsel), per_group(bwin), full(et_mat), full(ov_mat)],
        out_specs=pl.BlockSpec((None, TQ, qw), lambda g, bb, i: (bb, i, g)),
        compiler_params=_cparams(("parallel", "parallel", "arbitrary")),
        name="nsa",
    )(p3, p3, p3, kcvc, bsel, bwin, et_mat, ov_mat)


C_TILES_PER_Q = TQC // TQ
DIFF_CHAIN_ROWS = 256
N_C_TILES = 16


def _diff_kernel(q_ref, k_ref, v_ref, bias_ref, lam_ref, gain_ref, o_ref, *, lam_init):
    i = pl.program_id(2)
    seq = k_ref.shape[0]
    lv = lam_ref[...]
    lam = (jnp.exp(jnp.sum(lv[0:1] * lv[1:2], axis=-1, keepdims=True))
           - jnp.exp(jnp.sum(lv[2:3] * lv[3:4], axis=-1, keepdims=True)) + lam_init)
    qs = _stack_halves([q_ref[...]])
    q_per_ext = EXT_STEP // TQC
    for v in range(seq // EXT_STEP):
        @pl.when(i // q_per_ext == v)
        def _(v=v):
            ext = (v + 1) * EXT_STEP
            b0 = (N_C_TILES - EXT_STEP // TQ) - C_TILES_PER_Q * i
            bias = jnp.concatenate([bias_ref[b0 + kk] for kk in range(ext // TQ)], axis=1)
            bias2 = jnp.concatenate([bias, bias], axis=0)
            o = jnp.concatenate(
                [_softmax_pv(_dot_nt(qs[r0:r0 + DIFF_CHAIN_ROWS], k_ref[0:ext, :]) + bias2[r0:r0 + DIFF_CHAIN_ROWS],
                             v_ref[0:ext, :]) for r0 in range(0, 2 * TQC, DIFF_CHAIN_ROWS)], axis=0)
            o = o[:TQC] - lam * o[TQC:]
            o_ref[...] = (_rms(o, gain_ref[...]) * (1.0 - lam_init)).astype(o_ref.dtype)


def _diff(p3, bias, lam_vecs, sub_gain, lam_init):
    b, s, _ = p3.shape
    dv = 2 * HEAD_DIM
    full = lambda a: pl.BlockSpec(a.shape, lambda h, bb, i: (0,) * a.ndim)
    return pl.pallas_call(
        functools.partial(_diff_kernel, lam_init=lam_init),
        out_shape=jax.ShapeDtypeStruct((b, s, C_HEADS * dv), BF16),
        grid=(C_HEADS, b, s // TQC),
        in_specs=[
            pl.BlockSpec((None, TQC, dv), lambda h, bb, i: (bb, i, U_C + h)),
            pl.BlockSpec((None, s, dv), lambda h, bb, i: (bb, 0, U_C + C_HEADS + h)),
            pl.BlockSpec((None, s, dv), lambda h, bb, i: (bb, 0, U_C + 2 * C_HEADS + h)),
            pl.BlockSpec((None,) + bias.shape[1:], lambda h, bb, i: (h, 0, 0, 0)),
            full(lam_vecs), full(sub_gain)],
        out_specs=pl.BlockSpec((None, TQC, dv), lambda h, bb, i: (bb, i, h)),
        compiler_params=_cparams(("parallel", "parallel", "arbitrary")),
        name="diff",
    )(p3, p3, p3, bias, lam_vecs, sub_gain)


def _mixout_kernel(x_ref, oa0_ref, oa1_ref, oa2_ref, l0_ref, l1_ref, l2_ref, ob_ref, oc_ref,
                   g0_ref, g1_ref, g2_ref, wa_ref, wb_ref, wc_ref, wo_ref, gain_ref, o_ref):
    planes = lambda ref: jnp.concatenate([ref[p] for p in range(ref.shape[0])], axis=1)
    lse = [planes(l0_ref), planes(l1_ref), planes(l2_ref)]
    m = jnp.maximum(jnp.maximum(lse[0], lse[1]), lse[2])
    e = [jnp.exp(v - m) for v in lse]
    den = e[0] + e[1] + e[2]
    oa = (e[0] / den) * planes(oa0_ref) + (e[1] / den) * planes(oa1_ref) + (e[2] / den) * planes(oa2_ref)
    mixed = (jax.nn.sigmoid(g0_ref[...].astype(F32)) * _dot(oa.astype(BF16), wa_ref[...])
             + jax.nn.sigmoid(g1_ref[...].astype(F32)) * _dot(ob_ref[...], wb_ref[...])
             + jax.nn.sigmoid(g2_ref[...].astype(F32)) * _dot(oc_ref[...], wc_ref[...]))
    y = _dot(mixed.astype(BF16), wo_ref[...])
    o_ref[...] = x_ref[...] + _rms(y, gain_ref[...])


def _mixout(x, oa, lse, ob, oc, p, wa, wb, wc, wo, gain, tm=256):
    n, d = x.shape
    const = lambda a: pl.BlockSpec(a.shape, lambda i: (0,) * a.ndim, pipeline_mode=pl.Buffered(1))
    row = lambda a: pl.BlockSpec((tm, a.shape[1]), lambda i: (i, 0))
    grp = lambda a: pl.BlockSpec((a.shape[0], tm, a.shape[2]), lambda i: (0, i, 0))
    gate = lambda br: pl.BlockSpec((tm, d), lambda i: (i, U_MGATE * LANES // d + br))
    return pl.pallas_call(
        _mixout_kernel,
        out_shape=jax.ShapeDtypeStruct((n, d), F32),
        grid=(n // tm,),
        in_specs=[row(x)] + [grp(a) for a in oa] + [grp(a) for a in lse] + [row(ob), row(oc),
                  gate(0), gate(1), gate(2), const(wa), const(wb), const(wc), const(wo), const(gain)],
        out_specs=row(x),
        compiler_params=_cparams(("parallel",)),
        name="mixout",
    )(x, *oa, *lse, ob, oc, p, p, p, wa, wb, wc, wo, gain)


def _mlp_kernel(x_ref, gpre_ref, wup_ref, wdn_ref, gpost_ref, o_ref, h_ref, acc_ref):
    f = pl.program_id(1)

    @pl.when(f == 0)
    def _():
        h_ref[...] = _rms(x_ref[...], gpre_ref[...]).astype(BF16)
        acc_ref[...] = jnp.zeros_like(acc_ref)

    u = jnp.square(jax.nn.relu(_dot(h_ref[...], wup_ref[...])))
    acc_ref[...] += _dot(u.astype(BF16), wdn_ref[...])

    @pl.when(f == pl.num_programs(1) - 1)
    def _():
        o_ref[...] = x_ref[...] + _rms(acc_ref[...], gpost_ref[...])


def _mlp(x, gpre, wup, wdn, gpost, tm=512, tf=1024):
    n, d = x.shape
    dff = wup.shape[1]
    return pl.pallas_call(
        _mlp_kernel,
        out_shape=jax.ShapeDtypeStruct((n, d), F32),
        grid=(n // tm, dff // tf),
        in_specs=[pl.BlockSpec((tm, d), lambda i, f: (i, 0)),
                  pl.BlockSpec((1, d), lambda i, f: (0, 0)),
                  pl.BlockSpec((d, tf), lambda i, f: (0, f)),
                  pl.BlockSpec((tf, d), lambda i, f: (f, 0)),
                  pl.BlockSpec((1, d), lambda i, f: (0, 0))],
        out_specs=pl.BlockSpec((tm, d), lambda i, f: (i, 0)),
        scratch_shapes=[pltpu.VMEM((tm, d), BF16), pltpu.VMEM((tm, d), F32)],
        compiler_params=_cparams(("parallel", "arbitrary")),
        name="mlp",
    )(x, gpre, wup, wdn, gpost)


def _cast_kernel(w_ref, s_ref, o_ref):
    o_ref[...] = (w_ref[...] * s_ref[...]).astype(o_ref.dtype)


def _cast_bf16(w, colscale=None, block_bytes=8 * 1024 * 1024):
    nl, rows, cols = w.shape
    tr = max(8, min(rows, block_bytes // (4 * cols) // 8 * 8))
    while rows % tr:
        tr -= 8
    scale = jnp.ones((1, cols), F32) if colscale is None else jnp.asarray(colscale, F32).reshape(1, cols)
    return pl.pallas_call(
        _cast_kernel,
        out_shape=jax.ShapeDtypeStruct(w.shape, BF16),
        grid=(nl, rows // tr),
        in_specs=[pl.BlockSpec((None, tr, cols), lambda l, i: (l, i, 0)),
                  pl.BlockSpec((1, cols), lambda l, i: (0, 0))],
        out_specs=pl.BlockSpec((None, tr, cols), lambda l, i: (l, i, 0)),
        compiler_params=_cparams(("parallel", "parallel")),
        name="cast",
    )(w, scale)


def _toeplitz_kernel(v_ref, o_ref):
    n_tiles, rows, _ = o_ref.shape
    x = jnp.broadcast_to(v_ref[...], (rows, v_ref.shape[-1]))
    t = pltpu.roll(x, 0, 1, stride=1, stride_axis=0)
    for k in range(n_tiles):
        o_ref[k] = t[:, k * TQ:(k + 1) * TQ]


def _toeplitz_tiles(per_dist, dist_of_m, rows, n_tiles):
    width = n_tiles * TQ
    w = width + rows
    m = np.arange(w)
    m = np.where(m >= width, m - w, m)
    dist = dist_of_m(m)
    neg_col = per_dist.shape[1]
    idx = np.where(dist >= 0, np.minimum(dist, neg_col - 1), neg_col)
    heads = per_dist.shape[0]
    vec = jnp.concatenate([per_dist, jnp.full((heads, 1), NEG, F32)], axis=1)[:, idx]
    return pl.pallas_call(
        _toeplitz_kernel,
        out_shape=jax.ShapeDtypeStruct((heads, n_tiles, rows, TQ), F32),
        grid=(heads,),
        in_specs=[pl.BlockSpec((None, 1, w), lambda h: (h, 0, 0))],
        out_specs=pl.BlockSpec((None, n_tiles, rows, TQ), lambda h: (h, 0, 0, 0)),
        compiler_params=_cparams(("parallel",)),
        name="toeplitz",
    )(vec.reshape(heads, 1, w))


def _t5_bucket(dist):
    exact = N_BUCKETS // 2
    d = jnp.maximum(dist, 0)
    logd = jnp.log(jnp.maximum(d, 1).astype(F32) / exact)
    far = exact + (logd / math.log(BIAS_MAX_DIST / exact) * (N_BUCKETS - exact)).astype(jnp.int32)
    return jnp.where(d < exact, d, jnp.minimum(far, N_BUCKETS - 1))


def _bias_tables(rel_bias, seq):
    per_dist = rel_bias[_t5_bucket(jnp.arange(seq + 1))].T * LOG2E
    n_a = len(DIL_GROUPS) * A_HEADS
    bias_a = []
    for g, (_, dil) in enumerate(DIL_GROUPS):
        t = _toeplitz_tiles(per_dist[g * A_HEADS:(g + 1) * A_HEADS],
                            lambda m, dil=dil: np.where((m >= 0) & (m <= TQ), dil * (TQ - m), -1), TQ, 2)
        if seq // (TQ * dil) > 1:
            with_prev = t.transpose(0, 2, 1, 3).reshape(A_HEADS * TQ, 2 * TQ)
            no_prev = jnp.where(np.arange(2 * TQ)[None, :] < TQ, NEG, with_prev)
            bias_a.append(jnp.stack([no_prev, with_prev]))
        else:
            bias_a.append(t[:, 1].reshape(A_HEADS * TQ, TQ))
    pb = per_dist[n_a:n_a + B_HEADS]
    off_s = (N_SEL_TILES - EXT_STEP // TQ - 1) * TQ
    bias_sel = _toeplitz_tiles(pb, lambda m: off_s - m, TQ, N_SEL_TILES)
    bias_win = _toeplitz_tiles(pb, lambda m: np.where(WIN - m < WIN, WIN - m, -1), TQ, N_WIN_TILES)
    off_c = (N_C_TILES - EXT_STEP // TQ) * TQ
    bias_c = _toeplitz_tiles(per_dist[n_a + B_HEADS:], lambda m: off_c - m, TQC, N_C_TILES)
    return bias_a, bias_sel, bias_win, bias_c


def _w_in_segments(d):
    widths = (A_UNITS * LANES, B_HEADS * HEAD_DIM, 6 * B_KV_GROUPS * HEAD_DIM, 3 * B_HEADS,
              3 * C_HEADS * 2 * HEAD_DIM, 3 * d)
    return tuple(int(v) for v in np.cumsum((0,) + widths))


def _w_in_colscale(d):
    o_a, o_bq, o_bkv, _, o_c, o_mg, end = _w_in_segments(d)
    colscale = np.ones((end,), np.float32)
    for lo, n_q in ((o_a, (o_bq - o_a) // 3), (o_bq, o_bkv - o_bq), (o_c, (o_mg - o_c) // 3)):
        colscale[lo:lo + n_q] = SCALE * LOG2E
    return colscale


def _pack_w_in(wb):
    d = wb.shape[0]
    _, o_bq, o_bkv, o_bg, o_c, o_mg, _ = _w_in_segments(d)
    a_w = o_bq
    a, bq, bkv, bg, cc, mg = (wb[:, :a_w], wb[:, o_bq:o_bkv], wb[:, o_bkv:o_bg], wb[:, o_bg:o_c],
                              wb[:, o_c:o_mg], wb[:, o_mg:])
    bkv = bkv.reshape(d, 6, B_KV_GROUPS, HEAD_DIM)
    order = (2, 2, 4, 4, 3, 3, 5, 5, 0, 1)
    bkv = jnp.stack([bkv[:, o_, g] for g in range(B_KV_GROUPS) for o_ in order], axis=1)
    bkv = bkv.reshape(d, B_KV_GROUPS * BKV_UNITS * LANES)
    bg = bg.reshape(d, B_KV_GROUPS, B_HPG, 3).transpose(0, 1, 3, 2).reshape(d, B_KV_GROUPS, 3 * B_HPG)
    bg = jnp.pad(bg, ((0, 0), (0, 0), (0, LANES - 3 * B_HPG))).reshape(d, B_KV_GROUPS * LANES)
    packed = jnp.concatenate([a, mg, bq, cc, bkv, bg], axis=1)
    assert packed.shape[1] == a_w + NCOL
    return packed


def kernel(x, rel_bias, w_in, cmp_k_pos, cmp_v_pos, cmp_k_w1, cmp_k_w2, cmp_v_w1, cmp_v_w2, diff_lambda, diff_norm, w_branch_a, w_branch_b, w_branch_c, w_out, norm_mix_pre, norm_mix_post, norm_mlp_pre, norm_mlp_post, w_up, w_down):
    bsz, seq, d = x.shape
    n = bsz * seq
    depth = w_in.shape[0]
    nch = seq // CMP_STRIDE
    assert seq % EXT_STEP == 0 and seq // TQ == N_SEL_TILES - EXT_STEP // TQ and (U_MGATE * LANES) % d == 0
    assert N_C_TILES == EXT_STEP // TQ + C_TILES_PER_Q * (seq // TQC - 1)

    bias_a, bias_sel, bias_win, bias_c = _bias_tables(rel_bias, seq)
    et_mat = jnp.asarray(np.arange(seq)[:, None] // SEL_LEN == np.arange(LANES)[None, :], BF16)
    starts = np.arange(nch)[:, None] * CMP_STRIDE
    bstart = np.arange(LANES)[None, :] * SEL_LEN
    ov_mat = jnp.asarray((starts < bstart + SEL_LEN) & (starts + CMP_LEN > bstart), BF16)

    w_in_b = _cast_bf16(w_in, _w_in_colscale(d))
    w_up_b, w_down_b, w_out_b = _cast_bf16(w_up), _cast_bf16(w_down), _cast_bf16(w_out)
    w_a_b, w_b_b, w_c_b = _cast_bf16(w_branch_a), _cast_bf16(w_branch_b), _cast_bf16(w_branch_c)

    xf = x.reshape(n, d)
    for l in range(depth):
        lam_init = 0.8 - 0.6 * math.exp(-0.3 * l)
        pa, p = _proj(xf, norm_mix_pre[l][None], _pack_w_in(w_in_b[l]))
        p3 = p.reshape(bsz, seq, NCOL)

        a_out = [_attn_a(pa, bias_a[g], g, dil, bsz, seq) for g, (_, dil) in enumerate(DIL_GROUPS)]
        oa = [o for o, _ in a_out]
        lse = [s for _, s in a_out]

        kvc = jnp.stack([p3[:, :, (U_BKV + g * BKV_UNITS + BKV_UNITS - 1) * LANES:(U_BKV + (g + 1) * BKV_UNITS) * LANES]
                         for g in range(B_KV_GROUPS)], axis=1)
        xc = kvc.reshape(bsz, B_KV_GROUPS, seq, 2, HEAD_DIM).transpose(0, 1, 3, 2, 4)
        xc = xc.reshape(bsz * B_KV_GROUPS, 2, nch, CMP_STRIDE * HEAD_DIM)
        half = CMP_STRIDE * HEAD_DIM
        pos = jnp.stack([cmp_k_pos[l], cmp_v_pos[l]]).reshape(2, 2, half)
        w1 = jnp.stack([cmp_k_w1[l], cmp_v_w1[l]]).reshape(2, 2, half, CMP_HIDDEN).astype(BF16)
        w2 = jnp.stack([cmp_k_w2[l], cmp_v_w2[l]]).astype(BF16)
        kcvc = _cmp(xc, pos, w1, w2)
        ob = _nsa(p3, kcvc, bias_sel, bias_win, et_mat, ov_mat).reshape(n, B_HEADS * HEAD_DIM)

        oc = _diff(p3, bias_c, diff_lambda[l], diff_norm[l][None], lam_init).reshape(n, C_HEADS * 2 * HEAD_DIM)

        xf = _mixout(xf, oa, lse, ob, oc, p, w_a_b[l], w_b_b[l], w_c_b[l], w_out_b[l], norm_mix_post[l][None])
        xf = _mlp(xf, norm_mlp_pre[l][None], w_up_b[l], w_down_b[l], norm_mlp_post[l][None])
    return xf.reshape(bsz, seq, d)
```

```python
import functools
import math

import jax
import jax.numpy as jnp
import numpy as np
from jax import lax
from jax.experimental import pallas as pl
from jax.experimental.pallas import tpu as pltpu

F32 = jnp.float32
BF16 = jnp.bfloat16

LANES = 128
HEAD_DIM = 64
SCALE = HEAD_DIM ** -0.5
LOG2E = math.log2(math.e)
LN2 = math.log(2.0)
NORM_EPS = 1e-6
NEG = -1e30
TINY = 1e-30
TQ = 128
TQC = 512
EXT_STEP = 512

DIL_GROUPS = ((128, 1), (512, 4), (2048, 16))
A_HEADS = 8
A_PAIRS = A_HEADS // 2
B_HEADS = 16
B_KV_GROUPS = 2
B_HPG = B_HEADS // B_KV_GROUPS
C_HEADS = 8
CMP_LEN, CMP_STRIDE, CMP_HIDDEN = 32, 16, 256
SEL_LEN, SEL_TOPK, SEL_LOCAL = 64, 16, 2
FORCED_SCORE = 1e6
WIN = 512
N_BUCKETS = 32
BIAS_MAX_DIST = 2048

A_UNITS = 3 * len(DIL_GROUPS) * A_PAIRS
U_MGATE = 0
U_BQ = 48
U_C = 56
U_BKV = 80
BKV_UNITS = 5
U_BGATE = 90
U_TOTAL = 92
NCOL = U_TOTAL * LANES

VMEM_LIMIT = 56 * 1024 * 1024


def _cparams(sem):
    return pltpu.CompilerParams(dimension_semantics=sem, vmem_limit_bytes=VMEM_LIMIT)


def _rms(x, gain):
    return x * lax.rsqrt(jnp.mean(x * x, axis=-1, keepdims=True) + NORM_EPS) * gain


def _dot_nt(a, b):
    return lax.dot_general(a, b, (((1,), (1,)), ((), ())), preferred_element_type=F32)


def _dot(a, b):
    return jnp.dot(a, b, preferred_element_type=F32)


def _lane_half(shape):
    return lax.broadcasted_iota(jnp.int32, shape, len(shape) - 1) < HEAD_DIM


def _stack_halves(tiles):
    low = _lane_half(tiles[0].shape)
    out = []
    for t in tiles:
        z = jnp.zeros_like(t)
        out += [jnp.where(low, t, z), jnp.where(low, z, t)]
    return jnp.concatenate(out, axis=0)


def _softmax_pv(s, v):
    m = jnp.max(s, axis=-1, keepdims=True)
    p = jnp.exp2(s - m)
    l = jnp.sum(p, axis=-1, keepdims=True)
    return _dot(p.astype(BF16), v) / l


def _proj_kernel(x_ref, g_ref, w_ref, oa_ref, o_ref, h_ref, *, n_a):
    j = pl.program_id(1)

    @pl.when(j == 0)
    def _():
        h_ref[...] = _rms(x_ref[...], g_ref[...]).astype(BF16)

    res = _dot(h_ref[...], w_ref[...])

    @pl.when(j < n_a)
    def _():
        for c in range(oa_ref.shape[0]):
            oa_ref[c] = res[:, c * LANES:(c + 1) * LANES]

    @pl.when(j >= n_a)
    def _():
        o_ref[...] = res.astype(o_ref.dtype)


def _proj(x, gain, w, tm=2048, tn=512):
    n, d = x.shape
    n_a = A_UNITS * LANES // tn
    upb = tn // LANES
    return pl.pallas_call(
        functools.partial(_proj_kernel, n_a=n_a),
        out_shape=(jax.ShapeDtypeStruct((A_UNITS, n, LANES), F32), jax.ShapeDtypeStruct((n, NCOL), BF16)),
        grid=(n // tm, w.shape[1] // tn),
        in_specs=[pl.BlockSpec((tm, d), lambda i, j: (i, 0), pipeline_mode=pl.Buffered(1)),
                  pl.BlockSpec((1, d), lambda i, j: (0, 0)),
                  pl.BlockSpec((d, tn), lambda i, j: (0, j))],
        out_specs=(pl.BlockSpec((upb, tm, LANES), lambda i, j: (jnp.minimum(j, n_a - 1), i, 0)),
                   pl.BlockSpec((tm, tn), lambda i, j: (i, jnp.maximum(j - n_a, 0)))),
        scratch_shapes=[pltpu.VMEM((tm, d), BF16)],
        compiler_params=_cparams(("parallel", "arbitrary")),
        name="proj",
    )(x, gain, w)


def _attn_a_kernel(*refs, dil, use_prev):
    if use_prev:
        q_ref, kc_ref, vc_ref, kp_ref, vp_ref, bias_ref, o_ref, lse_ref = refs
    else:
        q_ref, kc_ref, vc_ref, bias_ref, o_ref, lse_ref = refs
    sb = pl.program_id(1)
    r = pl.program_id(2)
    rows = pl.ds(r, TQ, stride=dil) if dil > 1 else slice(None)
    ld = lambda ref, p: ref[p, rows, :].astype(BF16)
    s = []
    for p in range(A_PAIRS):
        qs = _stack_halves([ld(q_ref, p)])
        ks = jnp.concatenate([ld(kp_ref, p), ld(kc_ref, p)], axis=0) if use_prev else ld(kc_ref, p)
        s.append(_dot_nt(qs, ks))
    bias = bias_ref[(sb > 0).astype(jnp.int32)] if use_prev else bias_ref[...]
    s = jnp.concatenate(s, axis=0) + bias
    m = jnp.max(s, axis=-1, keepdims=True)
    e = jnp.exp2(s - m)
    l = jnp.sum(e, axis=-1, keepdims=True)
    pb = e.astype(BF16)
    lse = (m + jnp.log2(l)) * LN2
    low = _lane_half((TQ, LANES))
    for p in range(A_PAIRS):
        r0 = 2 * p * TQ
        vs = jnp.concatenate([ld(vp_ref, p), ld(vc_ref, p)], axis=0) if use_prev else ld(vc_ref, p)
        o = _dot(pb[r0:r0 + 2 * TQ], vs) / l[r0:r0 + 2 * TQ]
        o_ref[p, rows, :] = jnp.where(low, o[:TQ], o[TQ:])
        lse_ref[p, rows, :] = jnp.where(low, lse[r0:r0 + TQ], lse[r0 + TQ:r0 + 2 * TQ])


def _attn_a(pa, bias, g, dil, bsz, seq):
    n = pa.shape[1]
    blk = TQ * dil
    nsb = seq // blk
    use_prev = nsb > 1
    n_grp = len(DIL_GROUPS)
    cur = lambda qkv: pl.BlockSpec((A_PAIRS, blk, LANES), lambda b, sb, r: (qkv * n_grp + g, b * nsb + sb, 0))
    prev = lambda qkv: pl.BlockSpec((A_PAIRS, blk, LANES),
                                    lambda b, sb, r: (qkv * n_grp + g, b * nsb + jnp.maximum(sb - 1, 0), 0))
    out = pl.BlockSpec((A_PAIRS, blk, LANES), lambda b, sb, r: (0, b * nsb + sb, 0))
    in_specs = [cur(0), cur(1), cur(2)] + ([prev(1), prev(2)] if use_prev else [])
    in_specs.append(pl.BlockSpec(bias.shape, lambda b, sb, r: (0,) * bias.ndim))
    args = [pa] * (len(in_specs) - 1) + [bias]
    return pl.pallas_call(
        functools.partial(_attn_a_kernel, dil=dil, use_prev=use_prev),
        out_shape=(jax.ShapeDtypeStruct((A_PAIRS, n, LANES), F32),) * 2,
        grid=(bsz, nsb, dil),
        in_specs=in_specs,
        out_specs=(out, out),
        compiler_params=_cparams(("parallel", "parallel", "arbitrary")),
        name=f"attn_a_d{dil}",
    )(*args)


def _cmp_kernel(x_ref, pos_ref, w1_ref, w2_ref, o_ref):
    for kv in range(2):
        x = x_ref[kv].astype(F32)
        top = (x + pos_ref[kv, 0:1, :]).astype(BF16)
        bot = (x + pos_ref[kv, 1:2, :]).astype(BF16)
        p1 = _dot(top, w1_ref[kv, 0])
        p2 = _dot(bot, w1_ref[kv, 1])
        hid = p1 + pltpu.roll(p2, p2.shape[0] - 1, 0)
        out = _dot(jax.nn.gelu(hid).astype(BF16), w2_ref[kv]).astype(o_ref.dtype)
        o_ref[kv] = jnp.concatenate([out, out], axis=1)


def _cmp(xc, pos, w1, w2):
    nb = xc.shape[0]
    nch, cw = xc.shape[2], xc.shape[3]
    full = lambda a: pl.BlockSpec(a.shape, lambda i: (0,) * a.ndim)
    return pl.pallas_call(
        _cmp_kernel,
        out_shape=jax.ShapeDtypeStruct((nb, 2, nch, LANES), BF16),
        grid=(nb,),
        in_specs=[pl.BlockSpec((None, 2, nch, cw), lambda i: (i, 0, 0, 0)),
                  full(pos), full(w1), full(w2)],
        out_specs=pl.BlockSpec((None, 2, nch, LANES), lambda i: (i, 0, 0, 0)),
        compiler_params=_cparams(("parallel",)),
        name="cmp",
    )(xc, pos, w1, w2)


N_SEL_TILES = 20
N_WIN_TILES = 9
NSA_CHAIN_HEADS = 4


def _nsa_kernel(q_ref, kv_ref, gate_ref, kcvc_ref, bsel_ref, bwin_ref, et_ref, ov_ref, o_ref):
    i = pl.program_id(2)
    seq = kv_ref.shape[0]
    rows = lax.broadcasted_iota(jnp.int32, (TQ, 1), 0)
    t = i * TQ + rows
    q = q_ref[...]
    qs = _stack_halves([q[:, p * LANES:(p + 1) * LANES] for p in range(B_HPG // 2)])
    gates = jax.nn.sigmoid(gate_ref[...].astype(F32))
    head = lambda a, h: a[h * TQ:(h + 1) * TQ]

    kc = kcvc_ref[0]
    vc = kcvc_ref[1]
    n_cmp = kc.shape[0]
    n_idx = lax.broadcasted_iota(jnp.int32, (1, n_cmp), 1)
    cmask = (n_idx * CMP_STRIDE + (CMP_LEN - 1)) <= t
    c_add = jnp.concatenate([jnp.where(cmask, 0.0, NEG)] * B_HPG, axis=0)
    c_mul = jnp.concatenate([cmask.astype(F32)] * B_HPG, axis=0)
    lc = _dot_nt(qs, kc) + c_add
    e = jnp.exp2(lc - jnp.max(lc, axis=-1, keepdims=True)) * c_mul
    p_cmp = e / jnp.maximum(jnp.sum(e, axis=-1, keepdims=True), TINY)
    o_cmp = _dot(p_cmp.astype(BF16), vc)

    def with_selection_mask():
        psum = head(p_cmp, 0)
        for h in range(1, B_HPG):
            psum = psum + head(p_cmp, h)
        hi = psum.astype(BF16)
        lo = (psum - hi.astype(F32)).astype(BF16)
        score = _dot(hi, ov_ref[...]) + _dot(lo, ov_ref[...])
        jb = lax.broadcasted_iota(jnp.int32, (1, LANES), 1)
        back = lax.shift_right_arithmetic(t, int(math.log2(SEL_LEN))) - jb
        forced = (jb == 0) | ((back >= 0) & (back < SEL_LOCAL))
        score = jnp.where(forced, FORCED_SCORE, score)
        score = jnp.where(jb * SEL_LEN <= t, score, NEG)
        rank = jnp.zeros((TQ, LANES), jnp.int32)
        for ii in range(seq // SEL_LEN):
            col = score[:, ii:ii + 1]
            beats = (col > score) | ((col == score) & (ii < jb))
            rank = rank + beats.astype(jnp.int32)
        sel_neg = jnp.where(rank < SEL_TOPK, 0.0, NEG).astype(BF16)
        return jnp.concatenate([qs, jnp.concatenate([sel_neg] * B_HPG, axis=0)], axis=1)

    n_wt = WIN // TQ + 1
    kt0 = jnp.maximum(i - WIN // TQ, 0)
    k0 = pl.multiple_of(kt0 * TQ, TQ)
    w0 = WIN // TQ - i + kt0
    k_w = kv_ref[pl.ds(k0, n_wt * TQ), LANES:2 * LANES]
    v_w = kv_ref[pl.ds(k0, n_wt * TQ), 3 * LANES:4 * LANES]

    def chains(q_all, k, v, bias_ref, first_tile, n_tiles):
        outs = []
        s_all = _dot_nt(q_all, k)
        for h0 in range(0, B_HPG, NSA_CHAIN_HEADS):
            bias = jnp.concatenate(
                [jnp.concatenate([bias_ref[h, first_tile + kk] for kk in range(n_tiles)], axis=1)
                 for h in range(h0, h0 + NSA_CHAIN_HEADS)], axis=0)
            outs.append(_softmax_pv(s_all[h0 * TQ:(h0 + NSA_CHAIN_HEADS) * TQ] + bias, v))
        return jnp.concatenate(outs, axis=0)

    o_win = chains(qs, k_w, v_w, bwin_ref, w0, n_wt)

    def finish(o_sel):
        low = _lane_half((TQ, LANES))
        outs = []
        for h in range(B_HPG):
            outs.append(gates[:, h:h + 1] * head(o_cmp, h)
                        + gates[:, B_HPG + h:B_HPG + h + 1] * head(o_sel, h)
                        + gates[:, 2 * B_HPG + h:2 * B_HPG + h + 1] * head(o_win, h))
        for p in range(B_HPG // 2):
            o_ref[:, p * LANES:(p + 1) * LANES] = jnp.where(low, outs[2 * p], outs[2 * p + 1]).astype(o_ref.dtype)

    tiles_per_ext = EXT_STEP // TQ
    for v in range(seq // EXT_STEP):
        @pl.when(i // tiles_per_ext == v)
        def _(v=v):
            ext = (v + 1) * EXT_STEP
            b0 = (N_SEL_TILES - tiles_per_ext - 1) - i
            if ext <= SEL_TOPK * SEL_LEN:
                q_all, ks = qs, kv_ref[0:ext, 0:LANES]
            else:
                q_all = with_selection_mask()
                ks = jnp.concatenate([kv_ref[0:ext, 0:LANES], et_ref[0:ext, :]], axis=1)
            finish(chains(q_all, ks, kv_ref[0:ext, 2 * LANES:3 * LANES], bsel_ref, b0, ext // TQ))


def _nsa(p3, kcvc, bsel, bwin, et_mat, ov_mat):
    b, s, _ = p3.shape
    nq = s // TQ
    qw = B_HPG * HEAD_DIM
    kvw = BKV_UNITS * LANES
    full = lambda a: pl.BlockSpec(a.shape, lambda g, bb, i: (0,) * a.ndim)
    per_group = lambda a: pl.BlockSpec((B_HPG,) + a.shape[1:], lambda g, bb, i: (g, 0, 0, 0),
                                       pipeline_mode=pl.Buffered(1))
    return pl.pallas_call(
        _nsa_kernel,
        out_shape=jax.ShapeDtypeStruct((b, s, B_HEADS * HEAD_DIM), BF16),
        grid=(B_KV_GROUPS, b, nq),
        in_specs=[
            pl.BlockSpec((None, TQ, qw), lambda g, bb, i: (bb, i, U_BQ * LANES // qw + g)),
            pl.BlockSpec((None, s, kvw), lambda g, bb, i: (bb, 0, U_BKV * LANES // kvw + g)),
            pl.BlockSpec((None, TQ, LANES), lambda g, bb, i: (bb, i, U_BGATE + g)),
            pl.BlockSpec((None, 2, kcvc.shape[2], LANES), lambda g, bb, i: (bb * B_KV_GROUPS + g, 0, 0, 0)),
            per_group(bsel), per_group(bwin), full(et_mat), full(ov_mat)],
        out_specs=pl.BlockSpec((None, TQ, qw), lambda g, bb, i: (bb, i, g)),
        compiler_params=_cparams(("parallel", "parallel", "arbitrary")),
        name="nsa",
    )(p3, p3, p3, kcvc, bsel, bwin, et_mat, ov_mat)


C_TILES_PER_Q = TQC // TQ
DIFF_CHAIN_ROWS = 256
N_C_TILES = 16


def _diff_kernel(q_ref, k_ref, v_ref, bias_ref, lam_ref, gain_ref, o_ref, *, lam_init):
    i = pl.program_id(2)
    seq = k_ref.shape[0]
    lv = lam_ref[...]
    lam = (jnp.exp(jnp.sum(lv[0:1] * lv[1:2], axis=-1, keepdims=True))
           - jnp.exp(jnp.sum(lv[2:3] * lv[3:4], axis=-1, keepdims=True)) + lam_init)
    qs = _stack_halves([q_ref[...]])
    q_per_ext = EXT_STEP // TQC
    for v in range(seq // EXT_STEP):
        @pl.when(i // q_per_ext == v)
        def _(v=v):
            ext = (v + 1) * EXT_STEP
            b0 = (N_C_TILES - EXT_STEP // TQ) - C_TILES_PER_Q * i
            bias = jnp.concatenate([bias_ref[b0 + kk] for kk in range(ext // TQ)], axis=1)
            bias2 = jnp.concatenate([bias, bias], axis=0)
            o = jnp.concatenate(
                [_softmax_pv(_dot_nt(qs[r0:r0 + DIFF_CHAIN_ROWS], k_ref[0:ext, :]) + bias2[r0:r0 + DIFF_CHAIN_ROWS],
                             v_ref[0:ext, :]) for r0 in range(0, 2 * TQC, DIFF_CHAIN_ROWS)], axis=0)
            o = o[:TQC] - lam * o[TQC:]
            o_ref[...] = (_rms(o, gain_ref[...]) * (1.0 - lam_init)).astype(o_ref.dtype)


def _diff(p3, bias, lam_vecs, sub_gain, lam_init):
    b, s, _ = p3.shape
    dv = 2 * HEAD_DIM
    full = lambda a: pl.BlockSpec(a.shape, lambda h, bb, i: (0,) * a.ndim)
    return pl.pallas_call(
        functools.partial(_diff_kernel, lam_init=lam_init),
        out_shape=jax.ShapeDtypeStruct((b, s, C_HEADS * dv), BF16),
        grid=(C_HEADS, b, s // TQC),
        in_specs=[
            pl.BlockSpec((None, TQC, dv), lambda h, bb, i: (bb, i, U_C + h)),
            pl.BlockSpec((None, s, dv), lambda h, bb, i: (bb, 0, U_C + C_HEADS + h)),
            pl.BlockSpec((None, s, dv), lambda h, bb, i: (bb, 0, U_C + 2 * C_HEADS + h)),
            pl.BlockSpec((None,) + bias.shape[1:], lambda h, bb, i: (h, 0, 0, 0)),
            full(lam_vecs), full(sub_gain)],
        out_specs=pl.BlockSpec((None, TQC, dv), lambda h, bb, i: (bb, i, h)),
        compiler_params=_cparams(("parallel", "parallel", "arbitrary")),
        name="diff",
    )(p3, p3, p3, bias, lam_vecs, sub_gain)


def _mixout_kernel(x_ref, oa0_ref, oa1_ref, oa2_ref, l0_ref, l1_ref, l2_ref, ob_ref, oc_ref,
                   g0_ref, g1_ref, g2_ref, wa_ref, wb_ref, wc_ref, wo_ref, gain_ref, o_ref):
    planes = lambda ref: jnp.concatenate([ref[p] for p in range(ref.shape[0])], axis=1)
    lse = [planes(l0_ref), planes(l1_ref), planes(l2_ref)]
    m = jnp.maximum(jnp.maximum(lse[0], lse[1]), lse[2])
    e = [jnp.exp(v - m) for v in lse]
    den = e[0] + e[1] + e[2]
    oa = (e[0] / den) * planes(oa0_ref) + (e[1] / den) * planes(oa1_ref) + (e[2] / den) * planes(oa2_ref)
    mixed = (jax.nn.sigmoid(g0_ref[...].astype(F32)) * _dot(oa.astype(BF16), wa_ref[...])
             + jax.nn.sigmoid(g1_ref[...].astype(F32)) * _dot(ob_ref[...], wb_ref[...])
             + jax.nn.sigmoid(g2_ref[...].astype(F32)) * _dot(oc_ref[...], wc_ref[...]))
    y = _dot(mixed.astype(BF16), wo_ref[...])
    o_ref[...] = x_ref[...] + _rms(y, gain_ref[...])


def _mixout(x, oa, lse, ob, oc, p, wa, wb, wc, wo, gain, tm=256):
    n, d = x.shape
    const = lambda a: pl.BlockSpec(a.shape, lambda i: (0,) * a.ndim, pipeline_mode=pl.Buffered(1))
    row = lambda a: pl.BlockSpec((tm, a.shape[1]), lambda i: (i, 0))
    grp = lambda a: pl.BlockSpec((a.shape[0], tm, a.shape[2]), lambda i: (0, i, 0))
    gate = lambda br: pl.BlockSpec((tm, d), lambda i: (i, U_MGATE * LANES // d + br))
    return pl.pallas_call(
        _mixout_kernel,
        out_shape=jax.ShapeDtypeStruct((n, d), F32),
        grid=(n // tm,),
        in_specs=[row(x)] + [grp(a) for a in oa] + [grp(a) for a in lse] + [row(ob), row(oc),
                  gate(0), gate(1), gate(2), const(wa), const(wb), const(wc), const(wo), const(gain)],
        out_specs=row(x),
        compiler_params=_cparams(("parallel",)),
        name="mixout",
    )(x, *oa, *lse, ob, oc, p, p, p, wa, wb, wc, wo, gain)


def _mlp_kernel(x_ref, gpre_ref, wup_ref, wdn_ref, gpost_ref, o_ref, h_ref, acc_ref):
    f = pl.program_id(1)

    @pl.when(f == 0)
    def _():
        h_ref[...] = _rms(x_ref[...], gpre_ref[...]).astype(BF16)
        acc_ref[...] = jnp.zeros_like(acc_ref)

    u = jnp.square(jax.nn.relu(_dot(h_ref[...], wup_ref[...])))
    acc_ref[...] += _dot(u.astype(BF16), wdn_ref[...])

    @pl.when(f == pl.num_programs(1) - 1)
    def _():
        o_ref[...] = x_ref[...] + _rms(acc_ref[...], gpost_ref[...])


def _mlp(x, gpre, wup, wdn, gpost, tm=512, tf=1024):
    n, d = x.shape
    dff = wup.shape[1]
    return pl.pallas_call(
        _mlp_kernel,
        out_shape=jax.ShapeDtypeStruct((n, d), F32),
        grid=(n // tm, dff // tf),
        in_specs=[pl.BlockSpec((tm, d), lambda i, f: (i, 0)),
                  pl.BlockSpec((1, d), lambda i, f: (0, 0)),
                  pl.BlockSpec((d, tf), lambda i, f: (0, f)),
                  pl.BlockSpec((tf, d), lambda i, f: (f, 0)),
                  pl.BlockSpec((1, d), lambda i, f: (0, 0))],
        out_specs=pl.BlockSpec((tm, d), lambda i, f: (i, 0)),
        scratch_shapes=[pltpu.VMEM((tm, d), BF16), pltpu.VMEM((tm, d), F32)],
        compiler_params=_cparams(("parallel", "arbitrary")),
        name="mlp",
    )(x, gpre, wup, wdn, gpost)


def _cast_kernel(w_ref, o_ref):
    o_ref[...] = w_ref[...].astype(o_ref.dtype)


def _cast_bf16(w, block_bytes=8 * 1024 * 1024):
    nl, rows, cols = w.shape
    tr = max(8, min(rows, block_bytes // (4 * cols) // 8 * 8))
    while rows % tr:
        tr -= 8
    return pl.pallas_call(
        _cast_kernel,
        out_shape=jax.ShapeDtypeStruct(w.shape, BF16),
        grid=(nl, rows // tr),
        in_specs=[pl.BlockSpec((None, tr, cols), lambda l, i: (l, i, 0))],
        out_specs=pl.BlockSpec((None, tr, cols), lambda l, i: (l, i, 0)),
        compiler_params=_cparams(("parallel", "parallel")),
        name="cast",
    )(w)


def _toeplitz_kernel(v_ref, o_ref):
    n_tiles, rows, _ = o_ref.shape
    x = jnp.broadcast_to(v_ref[...], (rows, v_ref.shape[-1]))
    t = pltpu.roll(x, 0, 1, stride=1, stride_axis=0)
    for k in range(n_tiles):
        o_ref[k] = t[:, k * TQ:(k + 1) * TQ]


def _toeplitz_tiles(per_dist, dist_of_m, rows, n_tiles):
    width = n_tiles * TQ
    w = width + rows
    m = np.arange(w)
    m = np.where(m >= width, m - w, m)
    dist = dist_of_m(m)
    neg_col = per_dist.shape[1]
    idx = np.where(dist >= 0, np.minimum(dist, neg_col - 1), neg_col)
    heads = per_dist.shape[0]
    vec = jnp.concatenate([per_dist, jnp.full((heads, 1), NEG, F32)], axis=1)[:, idx]
    return pl.pallas_call(
        _toeplitz_kernel,
        out_shape=jax.ShapeDtypeStruct((heads, n_tiles, rows, TQ), F32),
        grid=(heads,),
        in_specs=[pl.BlockSpec((None, 1, w), lambda h: (h, 0, 0))],
        out_specs=pl.BlockSpec((None, n_tiles, rows, TQ), lambda h: (h, 0, 0, 0)),
        compiler_params=_cparams(("parallel",)),
        name="toeplitz",
    )(vec.reshape(heads, 1, w))


def _t5_bucket(dist):
    exact = N_BUCKETS // 2
    d = jnp.maximum(dist, 0)
    logd = jnp.log(jnp.maximum(d, 1).astype(F32) / exact)
    far = exact + (logd / math.log(BIAS_MAX_DIST / exact) * (N_BUCKETS - exact)).astype(jnp.int32)
    return jnp.where(d < exact, d, jnp.minimum(far, N_BUCKETS - 1))


def _bias_tables(rel_bias, seq):
    per_dist = rel_bias[_t5_bucket(jnp.arange(seq + 1))].T * LOG2E
    n_a = len(DIL_GROUPS) * A_HEADS
    bias_a = []
    for g, (_, dil) in enumerate(DIL_GROUPS):
        t = _toeplitz_tiles(per_dist[g * A_HEADS:(g + 1) * A_HEADS],
                            lambda m, dil=dil: np.where((m >= 0) & (m <= TQ), dil * (TQ - m), -1), TQ, 2)
        if seq // (TQ * dil) > 1:
            with_prev = t.transpose(0, 2, 1, 3).reshape(A_HEADS * TQ, 2 * TQ)
            no_prev = jnp.where(np.arange(2 * TQ)[None, :] < TQ, NEG, with_prev)
            bias_a.append(jnp.stack([no_prev, with_prev]))
        else:
            bias_a.append(t[:, 1].reshape(A_HEADS * TQ, TQ))
    pb = per_dist[n_a:n_a + B_HEADS]
    off_s = (N_SEL_TILES - EXT_STEP // TQ - 1) * TQ
    bias_sel = _toeplitz_tiles(pb, lambda m: off_s - m, TQ, N_SEL_TILES)
    bias_win = _toeplitz_tiles(pb, lambda m: np.where(WIN - m < WIN, WIN - m, -1), TQ, N_WIN_TILES)
    off_c = (N_C_TILES - EXT_STEP // TQ) * TQ
    bias_c = _toeplitz_tiles(per_dist[n_a + B_HEADS:], lambda m: off_c - m, TQC, N_C_TILES)
    return bias_a, bias_sel, bias_win, bias_c


def _w_in_segments(d):
    widths = (A_UNITS * LANES, B_HEADS * HEAD_DIM, 6 * B_KV_GROUPS * HEAD_DIM, 3 * B_HEADS,
              3 * C_HEADS * 2 * HEAD_DIM, 3 * d)
    return tuple(int(v) for v in np.cumsum((0,) + widths))


def _w_in_colscale(d):
    o_a, o_bq, o_bkv, _, o_c, o_mg, end = _w_in_segments(d)
    colscale = np.ones((end,), np.float32)
    for lo, n_q in ((o_a, (o_bq - o_a) // 3), (o_bq, o_bkv - o_bq), (o_c, (o_mg - o_c) // 3)):
        colscale[lo:lo + n_q] = SCALE * LOG2E
    return colscale


def _cast_pack_kernel(w_ref, s_ref, pkv_ref, pg_ref, o_ref, *, seg):
    _, o_bq, o_bkv, o_bg, o_c, o_mg, _ = seg
    w = (w_ref[...] * s_ref[...]).astype(BF16)
    bkv = _dot(w[:, o_bkv:o_bg], pkv_ref[...]).astype(BF16)
    bg = _dot(w[:, o_bg:o_bg + LANES], pg_ref[...]).astype(BF16)
    o_ref[...] = jnp.concatenate([w[:, :o_bq], w[:, o_mg:], w[:, o_bq:o_bkv], w[:, o_c:o_mg], bkv, bg], axis=1)


def _cast_pack_w_in(w_in, tr=128):
    nl, d, cols = w_in.shape
    seg = _w_in_segments(d)
    _, o_bq, o_bkv, o_bg, o_c, _, _ = seg
    order = (2, 2, 4, 4, 3, 3, 5, 5, 0, 1)
    pkv = np.zeros((o_bg - o_bkv, B_KV_GROUPS * BKV_UNITS * LANES), np.float32)
    for g in range(B_KV_GROUPS):
        for slot, src in enumerate(order):
            for c in range(HEAD_DIM):
                pkv[(src * B_KV_GROUPS + g) * HEAD_DIM + c, (g * len(order) + slot) * HEAD_DIM + c] = 1.0
    pg = np.zeros((LANES, B_KV_GROUPS * LANES), np.float32)
    for g in range(B_KV_GROUPS):
        for h in range(B_HPG):
            for br in range(3):
                pg[(g * B_HPG + h) * 3 + br, g * LANES + br * B_HPG + h] = 1.0
    scale = jnp.asarray(_w_in_colscale(d)).reshape(1, cols)
    full = lambda a: pl.BlockSpec(a.shape, lambda l, i: (0,) * a.ndim)
    pkv, pg = jnp.asarray(pkv, BF16), jnp.asarray(pg, BF16)
    return pl.pallas_call(
        functools.partial(_cast_pack_kernel, seg=seg),
        out_shape=jax.ShapeDtypeStruct((nl, d, o_bq + NCOL), BF16),
        grid=(nl, d // tr),
        in_specs=[pl.BlockSpec((None, tr, cols), lambda l, i: (l, i, 0)), full(scale), full(pkv), full(pg)],
        out_specs=pl.BlockSpec((None, tr, o_bq + NCOL), lambda l, i: (l, i, 0)),
        compiler_params=_cparams(("parallel", "parallel")),
        name="cast_pack",
    )(w_in, scale, pkv, pg)


def kernel(x, rel_bias, w_in, cmp_k_pos, cmp_v_pos, cmp_k_w1, cmp_k_w2, cmp_v_w1, cmp_v_w2, diff_lambda, diff_norm, w_branch_a, w_branch_b, w_branch_c, w_out, norm_mix_pre, norm_mix_post, norm_mlp_pre, norm_mlp_post, w_up, w_down):
    bsz, seq, d = x.shape
    n = bsz * seq
    depth = w_in.shape[0]
    nch = seq // CMP_STRIDE
    assert seq % EXT_STEP == 0 and seq // TQ == N_SEL_TILES - EXT_STEP // TQ and (U_MGATE * LANES) % d == 0
    assert N_C_TILES == EXT_STEP // TQ + C_TILES_PER_Q * (seq // TQC - 1)

    bias_a, bias_sel, bias_win, bias_c = _bias_tables(rel_bias, seq)
    et_mat = jnp.asarray(np.arange(seq)[:, None] // SEL_LEN == np.arange(LANES)[None, :], BF16)
    starts = np.arange(nch)[:, None] * CMP_STRIDE
    bstart = np.arange(LANES)[None, :] * SEL_LEN
    ov_mat = jnp.asarray((starts < bstart + SEL_LEN) & (starts + CMP_LEN > bstart), BF16)

    w_in_b = _cast_pack_w_in(w_in)
    w_up_b, w_down_b, w_out_b = _cast_bf16(w_up), _cast_bf16(w_down), _cast_bf16(w_out)
    w_a_b, w_b_b, w_c_b = _cast_bf16(w_branch_a), _cast_bf16(w_branch_b), _cast_bf16(w_branch_c)

    xf = x.reshape(n, d)
    for l in range(depth):
        lam_init = 0.8 - 0.6 * math.exp(-0.3 * l)
        pa, p = _proj(xf, norm_mix_pre[l][None], w_in_b[l])
        p3 = p.reshape(bsz, seq, NCOL)

        a_out = [_attn_a(pa, bias_a[g], g, dil, bsz, seq) for g, (_, dil) in enumerate(DIL_GROUPS)]
        oa = [o for o, _ in a_out]
        lse = [s for _, s in a_out]

        kvc = jnp.stack([p3[:, :, (U_BKV + g * BKV_UNITS + BKV_UNITS - 1) * LANES:(U_BKV + (g + 1) * BKV_UNITS) * LANES]
                         for g in range(B_KV_GROUPS)], axis=1)
        xc = kvc.reshape(bsz, B_KV_GROUPS, seq, 2, HEAD_DIM).transpose(0, 1, 3, 2, 4)
        xc = xc.reshape(bsz * B_KV_GROUPS, 2, nch, CMP_STRIDE * HEAD_DIM)
        half = CMP_STRIDE * HEAD_DIM
        pos = jnp.stack([cmp_k_pos[l], cmp_v_pos[l]]).reshape(2, 2, half)
        w1 = jnp.stack([cmp_k_w1[l], cmp_v_w1[l]]).reshape(2, 2, half, CMP_HIDDEN).astype(BF16)
        w2 = jnp.stack([cmp_k_w2[l], cmp_v_w2[l]]).astype(BF16)
        kcvc = _cmp(xc, pos, w1, w2)
        ob = _nsa(p3, kcvc, bias_sel, bias_win, et_mat, ov_mat).reshape(n, B_HEADS * HEAD_DIM)

        oc = _diff(p3, bias_c, diff_lambda[l], diff_norm[l][None], lam_init).reshape(n, C_HEADS * 2 * HEAD_DIM)

        xf = _mixout(xf, oa, lse, ob, oc, p, w_a_b[l], w_b_b[l], w_c_b[l], w_out_b[l], norm_mix_post[l][None])
        xf = _mlp(xf, norm_mlp_pre[l][None], w_up_b[l], w_down_b[l], norm_mlp_post[l][None])
    return xf.reshape(bsz, seq, d)
```

```python
import functools
import math

import jax
import jax.numpy as jnp
import numpy as np
from jax import lax
from jax.experimental import pallas as pl
from jax.experimental.pallas import tpu as pltpu

F32 = jnp.float32
BF16 = jnp.bfloat16

LANES = 128
HEAD_DIM = 64
SCALE = HEAD_DIM ** -0.5
LOG2E = math.log2(math.e)
LN2 = math.log(2.0)
NORM_EPS = 1e-6
NEG = -1e30
TINY = 1e-30
TQ = 128
TQC = 512
EXT_STEP = 512

DIL_GROUPS = ((128, 1), (512, 4), (2048, 16))
A_HEADS = 8
A_PAIRS = A_HEADS // 2
B_HEADS = 16
B_KV_GROUPS = 2
B_HPG = B_HEADS // B_KV_GROUPS
C_HEADS = 8
CMP_LEN, CMP_STRIDE, CMP_HIDDEN = 32, 16, 256
SEL_LEN, SEL_TOPK, SEL_LOCAL = 64, 16, 2
FORCED_SCORE = 1e6
WIN = 512
N_BUCKETS = 32
BIAS_MAX_DIST = 2048

A_UNITS = 3 * len(DIL_GROUPS) * A_PAIRS
U_MGATE = 0
U_BQ = 48
U_C = 56
U_BKV = 80
BKV_UNITS = 5
U_BGATE = 90
U_TOTAL = 92
NCOL = U_TOTAL * LANES

VMEM_LIMIT = 56 * 1024 * 1024


def _cparams(sem):
    return pltpu.CompilerParams(dimension_semantics=sem, vmem_limit_bytes=VMEM_LIMIT)


def _rms(x, gain):
    return x * lax.rsqrt(jnp.mean(x * x, axis=-1, keepdims=True) + NORM_EPS) * gain


def _dot_nt(a, b):
    return lax.dot_general(a, b, (((1,), (1,)), ((), ())), preferred_element_type=F32)


def _dot(a, b):
    return jnp.dot(a, b, preferred_element_type=F32)


def _lane_half(shape):
    return lax.broadcasted_iota(jnp.int32, shape, len(shape) - 1) < HEAD_DIM


def _stack_halves(tiles):
    low = _lane_half(tiles[0].shape)
    out = []
    for t in tiles:
        z = jnp.zeros_like(t)
        out += [jnp.where(low, t, z), jnp.where(low, z, t)]
    return jnp.concatenate(out, axis=0)


def _softmax_pv(s, v):
    m = jnp.max(s, axis=-1, keepdims=True)
    p = jnp.exp2(s - m)
    l = jnp.sum(p, axis=-1, keepdims=True)
    return _dot(p.astype(BF16), v) / l


def _proj_kernel(x_ref, g_ref, w_ref, oa_ref, o_ref, h_ref, *, n_a):
    j = pl.program_id(1)

    @pl.when(j == 0)
    def _():
        h_ref[...] = _rms(x_ref[...], g_ref[...]).astype(BF16)

    res = _dot(h_ref[...], w_ref[...])

    @pl.when(j < n_a)
    def _():
        for c in range(oa_ref.shape[0]):
            oa_ref[c] = res[:, c * LANES:(c + 1) * LANES]

    @pl.when(j >= n_a)
    def _():
        o_ref[...] = res.astype(o_ref.dtype)


def _proj(x, gain, w, layer, tm=2048, tn=512):
    n, d = x.shape
    n_a = A_UNITS * LANES // tn
    upb = tn // LANES
    return pl.pallas_call(
        functools.partial(_proj_kernel, n_a=n_a),
        out_shape=(jax.ShapeDtypeStruct((A_UNITS, n, LANES), F32), jax.ShapeDtypeStruct((n, NCOL), BF16)),
        grid=(n // tm, w.shape[2] // tn),
        in_specs=[pl.BlockSpec((tm, d), lambda i, j: (i, 0), pipeline_mode=pl.Buffered(1)),
                  pl.BlockSpec((1, d), lambda i, j: (0, 0)),
                  pl.BlockSpec((None, d, tn), lambda i, j: (layer, 0, j))],
        out_specs=(pl.BlockSpec((upb, tm, LANES), lambda i, j: (jnp.minimum(j, n_a - 1), i, 0)),
                   pl.BlockSpec((tm, tn), lambda i, j: (i, jnp.maximum(j - n_a, 0)))),
        scratch_shapes=[pltpu.VMEM((tm, d), BF16)],
        compiler_params=_cparams(("parallel", "arbitrary")),
        name="proj",
    )(x, gain, w)


def _attn_a_kernel(*refs, dil, use_prev):
    if use_prev:
        q_ref, kc_ref, vc_ref, kp_ref, vp_ref, bias_ref, o_ref, lse_ref = refs
    else:
        q_ref, kc_ref, vc_ref, bias_ref, o_ref, lse_ref = refs
    sb = pl.program_id(1)
    r = pl.program_id(2)
    rows = pl.ds(r, TQ, stride=dil) if dil > 1 else slice(None)
    ld = lambda ref, p: ref[p, rows, :].astype(BF16)
    s = []
    for p in range(A_PAIRS):
        qs = _stack_halves([ld(q_ref, p)])
        ks = jnp.concatenate([ld(kp_ref, p), ld(kc_ref, p)], axis=0) if use_prev else ld(kc_ref, p)
        s.append(_dot_nt(qs, ks))
    bias = bias_ref[(sb > 0).astype(jnp.int32)] if use_prev else bias_ref[...]
    s = jnp.concatenate(s, axis=0) + bias
    m = jnp.max(s, axis=-1, keepdims=True)
    e = jnp.exp2(s - m)
    l = jnp.sum(e, axis=-1, keepdims=True)
    pb = e.astype(BF16)
    lse = (m + jnp.log2(l)) * LN2
    low = _lane_half((TQ, LANES))
    for p in range(A_PAIRS):
        r0 = 2 * p * TQ
        vs = jnp.concatenate([ld(vp_ref, p), ld(vc_ref, p)], axis=0) if use_prev else ld(vc_ref, p)
        o = _dot(pb[r0:r0 + 2 * TQ], vs) / l[r0:r0 + 2 * TQ]
        o_ref[p, rows, :] = jnp.where(low, o[:TQ], o[TQ:])
        lse_ref[p, rows, :] = jnp.where(low, lse[r0:r0 + TQ], lse[r0 + TQ:r0 + 2 * TQ])


def _attn_a(pa, bias, g, dil, bsz, seq):
    n = pa.shape[1]
    blk = TQ * dil
    nsb = seq // blk
    use_prev = nsb > 1
    n_grp = len(DIL_GROUPS)
    cur = lambda qkv: pl.BlockSpec((A_PAIRS, blk, LANES), lambda b, sb, r: (qkv * n_grp + g, b * nsb + sb, 0))
    prev = lambda qkv: pl.BlockSpec((A_PAIRS, blk, LANES),
                                    lambda b, sb, r: (qkv * n_grp + g, b * nsb + jnp.maximum(sb - 1, 0), 0))
    out = pl.BlockSpec((A_PAIRS, blk, LANES), lambda b, sb, r: (0, b * nsb + sb, 0))
    in_specs = [cur(0), cur(1), cur(2)] + ([prev(1), prev(2)] if use_prev else [])
    in_specs.append(pl.BlockSpec(bias.shape, lambda b, sb, r: (0,) * bias.ndim))
    args = [pa] * (len(in_specs) - 1) + [bias]
    return pl.pallas_call(
        functools.partial(_attn_a_kernel, dil=dil, use_prev=use_prev),
        out_shape=(jax.ShapeDtypeStruct((A_PAIRS, n, LANES), F32),) * 2,
        grid=(bsz, nsb, dil),
        in_specs=in_specs,
        out_specs=(out, out),
        compiler_params=_cparams(("parallel", "parallel", "arbitrary")),
        name=f"attn_a_d{dil}",
    )(*args)


def _cmp_kernel(x_ref, pos_ref, w1_ref, w2_ref, o_ref):
    for kv in range(2):
        x = x_ref[kv].astype(F32)
        top = (x + pos_ref[kv, 0:1, :]).astype(BF16)
        bot = (x + pos_ref[kv, 1:2, :]).astype(BF16)
        p1 = _dot(top, w1_ref[kv, 0])
        p2 = _dot(bot, w1_ref[kv, 1])
        hid = p1 + pltpu.roll(p2, p2.shape[0] - 1, 0)
        out = _dot(jax.nn.gelu(hid).astype(BF16), w2_ref[kv]).astype(o_ref.dtype)
        o_ref[kv] = jnp.concatenate([out, out], axis=1)


def _cmp(xc, pos, w1, w2):
    nb = xc.shape[0]
    nch, cw = xc.shape[2], xc.shape[3]
    full = lambda a: pl.BlockSpec(a.shape, lambda i: (0,) * a.ndim)
    return pl.pallas_call(
        _cmp_kernel,
        out_shape=jax.ShapeDtypeStruct((nb, 2, nch, LANES), BF16),
        grid=(nb,),
        in_specs=[pl.BlockSpec((None, 2, nch, cw), lambda i: (i, 0, 0, 0)),
                  full(pos), full(w1), full(w2)],
        out_specs=pl.BlockSpec((None, 2, nch, LANES), lambda i: (i, 0, 0, 0)),
        compiler_params=_cparams(("parallel",)),
        name="cmp",
    )(xc, pos, w1, w2)


N_SEL_TILES = 20
N_WIN_TILES = 9
NSA_CHAIN_HEADS = 4


def _nsa_kernel(q_ref, kv_ref, gate_ref, kcvc_ref, bsel_ref, bwin_ref, et_ref, ov_ref, o_ref):
    i = pl.program_id(2)
    seq = kv_ref.shape[0]
    rows = lax.broadcasted_iota(jnp.int32, (TQ, 1), 0)
    t = i * TQ + rows
    q = q_ref[...]
    qs = _stack_halves([q[:, p * LANES:(p + 1) * LANES] for p in range(B_HPG // 2)])
    gates = jax.nn.sigmoid(gate_ref[...].astype(F32))
    head = lambda a, h: a[h * TQ:(h + 1) * TQ]

    kc = kcvc_ref[0]
    vc = kcvc_ref[1]
    n_cmp = kc.shape[0]
    n_idx = lax.broadcasted_iota(jnp.int32, (1, n_cmp), 1)
    cmask = (n_idx * CMP_STRIDE + (CMP_LEN - 1)) <= t
    c_add = jnp.concatenate([jnp.where(cmask, 0.0, NEG)] * B_HPG, axis=0)
    c_mul = jnp.concatenate([cmask.astype(F32)] * B_HPG, axis=0)
    lc = _dot_nt(qs, kc) + c_add
    e = jnp.exp2(lc - jnp.max(lc, axis=-1, keepdims=True)) * c_mul
    p_cmp = e / jnp.maximum(jnp.sum(e, axis=-1, keepdims=True), TINY)
    o_cmp = _dot(p_cmp.astype(BF16), vc)

    def with_selection_mask():
        psum = head(p_cmp, 0)
        for h in range(1, B_HPG):
            psum = psum + head(p_cmp, h)
        hi = psum.astype(BF16)
        lo = (psum - hi.astype(F32)).astype(BF16)
        score = _dot(hi, ov_ref[...]) + _dot(lo, ov_ref[...])
        jb = lax.broadcasted_iota(jnp.int32, (1, LANES), 1)
        back = lax.shift_right_arithmetic(t, int(math.log2(SEL_LEN))) - jb
        forced = (jb == 0) | ((back >= 0) & (back < SEL_LOCAL))
        score = jnp.where(forced, FORCED_SCORE, score)
        score = jnp.where(jb * SEL_LEN <= t, score, NEG)
        rank = jnp.zeros((TQ, LANES), jnp.int32)
        for ii in range(seq // SEL_LEN):
            col = score[:, ii:ii + 1]
            beats = (col > score) | ((col == score) & (ii < jb))
            rank = rank + beats.astype(jnp.int32)
        sel_neg = jnp.where(rank < SEL_TOPK, 0.0, NEG).astype(BF16)
        return jnp.concatenate([qs, jnp.concatenate([sel_neg] * B_HPG, axis=0)], axis=1)

    n_wt = WIN // TQ + 1
    kt0 = jnp.maximum(i - WIN // TQ, 0)
    k0 = pl.multiple_of(kt0 * TQ, TQ)
    w0 = WIN // TQ - i + kt0
    k_w = kv_ref[pl.ds(k0, n_wt * TQ), LANES:2 * LANES]
    v_w = kv_ref[pl.ds(k0, n_wt * TQ), 3 * LANES:4 * LANES]

    def chains(q_all, k, v, bias_ref, first_tile, n_tiles):
        outs = []
        s_all = _dot_nt(q_all, k)
        for h0 in range(0, B_HPG, NSA_CHAIN_HEADS):
            bias = jnp.concatenate(
                [jnp.concatenate([bias_ref[h, first_tile + kk] for kk in range(n_tiles)], axis=1)
                 for h in range(h0, h0 + NSA_CHAIN_HEADS)], axis=0)
            outs.append(_softmax_pv(s_all[h0 * TQ:(h0 + NSA_CHAIN_HEADS) * TQ] + bias, v))
        return jnp.concatenate(outs, axis=0)

    o_win = chains(qs, k_w, v_w, bwin_ref, w0, n_wt)

    def finish(o_sel):
        low = _lane_half((TQ, LANES))
        outs = []
        for h in range(B_HPG):
            outs.append(gates[:, h:h + 1] * head(o_cmp, h)
                        + gates[:, B_HPG + h:B_HPG + h + 1] * head(o_sel, h)
                        + gates[:, 2 * B_HPG + h:2 * B_HPG + h + 1] * head(o_win, h))
        for p in range(B_HPG // 2):
            o_ref[:, p * LANES:(p + 1) * LANES] = jnp.where(low, outs[2 * p], outs[2 * p + 1]).astype(o_ref.dtype)

    tiles_per_ext = EXT_STEP // TQ
    for v in range(seq // EXT_STEP):
        @pl.when(i // tiles_per_ext == v)
        def _(v=v):
            ext = (v + 1) * EXT_STEP
            b0 = (N_SEL_TILES - tiles_per_ext - 1) - i
            if ext <= SEL_TOPK * SEL_LEN:
                q_all, ks = qs, kv_ref[0:ext, 0:LANES]
            else:
                q_all = with_selection_mask()
                ks = jnp.concatenate([kv_ref[0:ext, 0:LANES], et_ref[0:ext, :]], axis=1)
            finish(chains(q_all, ks, kv_ref[0:ext, 2 * LANES:3 * LANES], bsel_ref, b0, ext // TQ))


def _nsa(p3, kcvc, bsel, bwin, et_mat, ov_mat):
    b, s, _ = p3.shape
    nq = s // TQ
    qw = B_HPG * HEAD_DIM
    kvw = BKV_UNITS * LANES
    full = lambda a: pl.BlockSpec(a.shape, lambda g, bb, i: (0,) * a.ndim)
    per_group = lambda a: pl.BlockSpec((B_HPG,) + a.shape[1:], lambda g, bb, i: (g, 0, 0, 0),
                                       pipeline_mode=pl.Buffered(1))
    return pl.pallas_call(
        _nsa_kernel,
        out_shape=jax.ShapeDtypeStruct((b, s, B_HEADS * HEAD_DIM), BF16),
        grid=(B_KV_GROUPS, b, nq),
        in_specs=[
            pl.BlockSpec((None, TQ, qw), lambda g, bb, i: (bb, i, U_BQ * LANES // qw + g)),
            pl.BlockSpec((None, s, kvw), lambda g, bb, i: (bb, 0, U_BKV * LANES // kvw + g)),
            pl.BlockSpec((None, TQ, LANES), lambda g, bb, i: (bb, i, U_BGATE + g)),
            pl.BlockSpec((None, 2, kcvc.shape[2], LANES), lambda g, bb, i: (bb * B_KV_GROUPS + g, 0, 0, 0)),
            per_group(bsel), per_group(bwin), full(et_mat), full(ov_mat)],
        out_specs=pl.BlockSpec((None, TQ, qw), lambda g, bb, i: (bb, i, g)),
        compiler_params=_cparams(("parallel", "parallel", "arbitrary")),
        name="nsa",
    )(p3, p3, p3, kcvc, bsel, bwin, et_mat, ov_mat)


C_TILES_PER_Q = TQC // TQ
DIFF_CHAIN_ROWS = 256
N_C_TILES = 16


def _diff_kernel(q_ref, k_ref, v_ref, bias_ref, lam_ref, gain_ref, o_ref, *, lam_init):
    i = pl.program_id(2)
    seq = k_ref.shape[0]
    lv = lam_ref[...]
    lam = (jnp.exp(jnp.sum(lv[0:1] * lv[1:2], axis=-1, keepdims=True))
           - jnp.exp(jnp.sum(lv[2:3] * lv[3:4], axis=-1, keepdims=True)) + lam_init)
    qs = _stack_halves([q_ref[...]])
    q_per_ext = EXT_STEP // TQC
    for v in range(seq // EXT_STEP):
        @pl.when(i // q_per_ext == v)
        def _(v=v):
            ext = (v + 1) * EXT_STEP
            b0 = (N_C_TILES - EXT_STEP // TQ) - C_TILES_PER_Q * i
            bias = jnp.concatenate([bias_ref[b0 + kk] for kk in range(ext // TQ)], axis=1)
            bias2 = jnp.concatenate([bias, bias], axis=0)
            o = jnp.concatenate(
                [_softmax_pv(_dot_nt(qs[r0:r0 + DIFF_CHAIN_ROWS], k_ref[0:ext, :]) + bias2[r0:r0 + DIFF_CHAIN_ROWS],
                             v_ref[0:ext, :]) for r0 in range(0, 2 * TQC, DIFF_CHAIN_ROWS)], axis=0)
            o = o[:TQC] - lam * o[TQC:]
            o_ref[...] = (_rms(o, gain_ref[...]) * (1.0 - lam_init)).astype(o_ref.dtype)


def _diff(p3, bias, lam_vecs, sub_gain, lam_init):
    b, s, _ = p3.shape
    dv = 2 * HEAD_DIM
    full = lambda a: pl.BlockSpec(a.shape, lambda h, bb, i: (0,) * a.ndim)
    return pl.pallas_call(
        functools.partial(_diff_kernel, lam_init=lam_init),
        out_shape=jax.ShapeDtypeStruct((b, s, C_HEADS * dv), BF16),
        grid=(C_HEADS, b, s // TQC),
        in_specs=[
            pl.BlockSpec((None, TQC, dv), lambda h, bb, i: (bb, i, U_C + h)),
            pl.BlockSpec((None, s, dv), lambda h, bb, i: (bb, 0, U_C + C_HEADS + h)),
            pl.BlockSpec((None, s, dv), lambda h, bb, i: (bb, 0, U_C + 2 * C_HEADS + h)),
            pl.BlockSpec((None,) + bias.shape[1:], lambda h, bb, i: (h, 0, 0, 0)),
            full(lam_vecs), full(sub_gain)],
        out_specs=pl.BlockSpec((None, TQC, dv), lambda h, bb, i: (bb, i, h)),
        compiler_params=_cparams(("parallel", "parallel", "arbitrary")),
        name="diff",
    )(p3, p3, p3, bias, lam_vecs, sub_gain)


def _mixout_kernel(x_ref, oa0_ref, oa1_ref, oa2_ref, l0_ref, l1_ref, l2_ref, ob_ref, oc_ref,
                   g0_ref, g1_ref, g2_ref, wa_ref, wb_ref, wc_ref, wo_ref, gain_ref, o_ref):
    planes = lambda ref: jnp.concatenate([ref[p] for p in range(ref.shape[0])], axis=1)
    lse = [planes(l0_ref), planes(l1_ref), planes(l2_ref)]
    m = jnp.maximum(jnp.maximum(lse[0], lse[1]), lse[2])
    e = [jnp.exp(v - m) for v in lse]
    den = e[0] + e[1] + e[2]
    oa = (e[0] / den) * planes(oa0_ref) + (e[1] / den) * planes(oa1_ref) + (e[2] / den) * planes(oa2_ref)
    mixed = (jax.nn.sigmoid(g0_ref[...].astype(F32)) * _dot(oa.astype(BF16), wa_ref[...])
             + jax.nn.sigmoid(g1_ref[...].astype(F32)) * _dot(ob_ref[...], wb_ref[...])
             + jax.nn.sigmoid(g2_ref[...].astype(F32)) * _dot(oc_ref[...], wc_ref[...]))
    y = _dot(mixed.astype(BF16), wo_ref[...])
    o_ref[...] = x_ref[...] + _rms(y, gain_ref[...])


def _mixout(x, oa, lse, ob, oc, p, wa, wb, wc, wo, gain, layer, tm=256):
    n, d = x.shape
    const = lambda a: pl.BlockSpec((None,) + a.shape[1:], lambda i: (layer,) + (0,) * (a.ndim - 1),
                                   pipeline_mode=pl.Buffered(1))
    row = lambda a: pl.BlockSpec((tm, a.shape[1]), lambda i: (i, 0))
    grp = lambda a: pl.BlockSpec((a.shape[0], tm, a.shape[2]), lambda i: (0, i, 0))
    gate = lambda br: pl.BlockSpec((tm, d), lambda i: (i, U_MGATE * LANES // d + br))
    return pl.pallas_call(
        _mixout_kernel,
        out_shape=jax.ShapeDtypeStruct((n, d), F32),
        grid=(n // tm,),
        in_specs=[row(x)] + [grp(a) for a in oa] + [grp(a) for a in lse] + [row(ob), row(oc),
                  gate(0), gate(1), gate(2), const(wa), const(wb), const(wc), const(wo), const(gain)],
        out_specs=row(x),
        compiler_params=_cparams(("parallel",)),
        name="mixout",
    )(x, *oa, *lse, ob, oc, p, p, p, wa, wb, wc, wo, gain)


def _mlp_kernel(x_ref, gpre_ref, wup_ref, wdn_ref, gpost_ref, o_ref, h_ref, acc_ref):
    f = pl.program_id(1)

    @pl.when(f == 0)
    def _():
        h_ref[...] = _rms(x_ref[...], gpre_ref[...]).astype(BF16)
        acc_ref[...] = jnp.zeros_like(acc_ref)

    u = jnp.square(jax.nn.relu(_dot(h_ref[...], wup_ref[...])))
    acc_ref[...] += _dot(u.astype(BF16), wdn_ref[...])

    @pl.when(f == pl.num_programs(1) - 1)
    def _():
        o_ref[...] = x_ref[...] + _rms(acc_ref[...], gpost_ref[...])


def _mlp(x, gpre, wup, wdn, gpost, layer, tm=512, tf=1024):
    n, d = x.shape
    dff = wup.shape[2]
    return pl.pallas_call(
        _mlp_kernel,
        out_shape=jax.ShapeDtypeStruct((n, d), F32),
        grid=(n // tm, dff // tf),
        in_specs=[pl.BlockSpec((tm, d), lambda i, f: (i, 0)),
                  pl.BlockSpec((1, d), lambda i, f: (0, 0)),
                  pl.BlockSpec((None, d, tf), lambda i, f: (layer, 0, f)),
                  pl.BlockSpec((None, tf, d), lambda i, f: (layer, f, 0)),
                  pl.BlockSpec((1, d), lambda i, f: (0, 0))],
        out_specs=pl.BlockSpec((tm, d), lambda i, f: (i, 0)),
        scratch_shapes=[pltpu.VMEM((tm, d), BF16), pltpu.VMEM((tm, d), F32)],
        compiler_params=_cparams(("parallel", "arbitrary")),
        name="mlp",
    )(x, gpre, wup, wdn, gpost)


def _cast_kernel(w_ref, o_ref):
    o_ref[...] = w_ref[...].astype(o_ref.dtype)


def _cast_bf16(w, block_bytes=8 * 1024 * 1024):
    nl, rows, cols = w.shape
    tr = max(8, min(rows, block_bytes // (4 * cols) // 8 * 8))
    while rows % tr:
        tr -= 8
    return pl.pallas_call(
        _cast_kernel,
        out_shape=jax.ShapeDtypeStruct(w.shape, BF16),
        grid=(nl, rows // tr),
        in_specs=[pl.BlockSpec((None, tr, cols), lambda l, i: (l, i, 0))],
        out_specs=pl.BlockSpec((None, tr, cols), lambda l, i: (l, i, 0)),
        compiler_params=_cparams(("parallel", "parallel")),
        name="cast",
    )(w)


def _toeplitz_kernel(v_ref, o_ref):
    n_tiles, rows, _ = o_ref.shape
    x = jnp.broadcast_to(v_ref[...], (rows, v_ref.shape[-1]))
    t = pltpu.roll(x, 0, 1, stride=1, stride_axis=0)
    for k in range(n_tiles):
        o_ref[k] = t[:, k * TQ:(k + 1) * TQ]


def _toeplitz_tiles(per_dist, dist_of_m, rows, n_tiles):
    width = n_tiles * TQ
    w = width + rows
    m = np.arange(w)
    m = np.where(m >= width, m - w, m)
    dist = dist_of_m(m)
    neg_col = per_dist.shape[1]
    idx = np.where(dist >= 0, np.minimum(dist, neg_col - 1), neg_col)
    heads = per_dist.shape[0]
    vec = jnp.concatenate([per_dist, jnp.full((heads, 1), NEG, F32)], axis=1)[:, idx]
    return pl.pallas_call(
        _toeplitz_kernel,
        out_shape=jax.ShapeDtypeStruct((heads, n_tiles, rows, TQ), F32),
        grid=(heads,),
        in_specs=[pl.BlockSpec((None, 1, w), lambda h: (h, 0, 0))],
        out_specs=pl.BlockSpec((None, n_tiles, rows, TQ), lambda h: (h, 0, 0, 0)),
        compiler_params=_cparams(("parallel",)),
        name="toeplitz",
    )(vec.reshape(heads, 1, w))


def _t5_bucket(dist):
    exact = N_BUCKETS // 2
    d = jnp.maximum(dist, 0)
    logd = jnp.log(jnp.maximum(d, 1).astype(F32) / exact)
    far = exact + (logd / math.log(BIAS_MAX_DIST / exact) * (N_BUCKETS - exact)).astype(jnp.int32)
    return jnp.where(d < exact, d, jnp.minimum(far, N_BUCKETS - 1))


def _bias_tables(rel_bias, seq):
    per_dist = rel_bias[_t5_bucket(jnp.arange(seq + 1))].T * LOG2E
    n_a = len(DIL_GROUPS) * A_HEADS
    bias_a = []
    for g, (_, dil) in enumerate(DIL_GROUPS):
        t = _toeplitz_tiles(per_dist[g * A_HEADS:(g + 1) * A_HEADS],
                            lambda m, dil=dil: np.where((m >= 0) & (m <= TQ), dil * (TQ - m), -1), TQ, 2)
        if seq // (TQ * dil) > 1:
            with_prev = t.transpose(0, 2, 1, 3).reshape(A_HEADS * TQ, 2 * TQ)
            no_prev = jnp.where(np.arange(2 * TQ)[None, :] < TQ, NEG, with_prev)
            bias_a.append(jnp.stack([no_prev, with_prev]))
        else:
            bias_a.append(t[:, 1].reshape(A_HEADS * TQ, TQ))
    pb = per_dist[n_a:n_a + B_HEADS]
    off_s = (N_SEL_TILES - EXT_STEP // TQ - 1) * TQ
    bias_sel = _toeplitz_tiles(pb, lambda m: off_s - m, TQ, N_SEL_TILES)
    bias_win = _toeplitz_tiles(pb, lambda m: np.where(WIN - m < WIN, WIN - m, -1), TQ, N_WIN_TILES)
    off_c = (N_C_TILES - EXT_STEP // TQ) * TQ
    bias_c = _toeplitz_tiles(per_dist[n_a + B_HEADS:], lambda m: off_c - m, TQC, N_C_TILES)
    return bias_a, bias_sel, bias_win, bias_c


def _w_in_segments(d):
    widths = (A_UNITS * LANES, B_HEADS * HEAD_DIM, 6 * B_KV_GROUPS * HEAD_DIM, 3 * B_HEADS,
              3 * C_HEADS * 2 * HEAD_DIM, 3 * d)
    return tuple(int(v) for v in np.cumsum((0,) + widths))


def _w_in_colscale(d):
    o_a, o_bq, o_bkv, _, o_c, o_mg, end = _w_in_segments(d)
    colscale = np.ones((end,), np.float32)
    for lo, n_q in ((o_a, (o_bq - o_a) // 3), (o_bq, o_bkv - o_bq), (o_c, (o_mg - o_c) // 3)):
        colscale[lo:lo + n_q] = SCALE * LOG2E
    return colscale


def _cast_pack_kernel(w_ref, s_ref, pkv_ref, pg_ref, o_ref, *, seg):
    _, o_bq, o_bkv, o_bg, o_c, o_mg, _ = seg
    w = (w_ref[...] * s_ref[...]).astype(BF16)
    bkv = _dot(w[:, o_bkv:o_bg], pkv_ref[...]).astype(BF16)
    bg = _dot(w[:, o_bg:o_bg + LANES], pg_ref[...]).astype(BF16)
    o_ref[...] = jnp.concatenate([w[:, :o_bq], w[:, o_mg:], w[:, o_bq:o_bkv], w[:, o_c:o_mg], bkv, bg], axis=1)


def _cast_pack_w_in(w_in, tr=128):
    nl, d, cols = w_in.shape
    seg = _w_in_segments(d)
    _, o_bq, o_bkv, o_bg, o_c, _, _ = seg
    order = (2, 2, 4, 4, 3, 3, 5, 5, 0, 1)
    pkv = np.zeros((o_bg - o_bkv, B_KV_GROUPS * BKV_UNITS * LANES), np.float32)
    for g in range(B_KV_GROUPS):
        for slot, src in enumerate(order):
            for c in range(HEAD_DIM):
                pkv[(src * B_KV_GROUPS + g) * HEAD_DIM + c, (g * len(order) + slot) * HEAD_DIM + c] = 1.0
    pg = np.zeros((LANES, B_KV_GROUPS * LANES), np.float32)
    for g in range(B_KV_GROUPS):
        for h in range(B_HPG):
            for br in range(3):
                pg[(g * B_HPG + h) * 3 + br, g * LANES + br * B_HPG + h] = 1.0
    scale = jnp.asarray(_w_in_colscale(d)).reshape(1, cols)
    full = lambda a: pl.BlockSpec(a.shape, lambda l, i: (0,) * a.ndim)
    pkv, pg = jnp.asarray(pkv, BF16), jnp.asarray(pg, BF16)
    return pl.pallas_call(
        functools.partial(_cast_pack_kernel, seg=seg),
        out_shape=jax.ShapeDtypeStruct((nl, d, o_bq + NCOL), BF16),
        grid=(nl, d // tr),
        in_specs=[pl.BlockSpec((None, tr, cols), lambda l, i: (l, i, 0)), full(scale), full(pkv), full(pg)],
        out_specs=pl.BlockSpec((None, tr, o_bq + NCOL), lambda l, i: (l, i, 0)),
        compiler_params=_cparams(("parallel", "parallel")),
        name="cast_pack",
    )(w_in, scale, pkv, pg)


def kernel(x, rel_bias, w_in, cmp_k_pos, cmp_v_pos, cmp_k_w1, cmp_k_w2, cmp_v_w1, cmp_v_w2, diff_lambda, diff_norm, w_branch_a, w_branch_b, w_branch_c, w_out, norm_mix_pre, norm_mix_post, norm_mlp_pre, norm_mlp_post, w_up, w_down):
    bsz, seq, d = x.shape
    n = bsz * seq
    depth = w_in.shape[0]
    nch = seq // CMP_STRIDE
    assert seq % EXT_STEP == 0 and seq // TQ == N_SEL_TILES - EXT_STEP // TQ and (U_MGATE * LANES) % d == 0
    assert N_C_TILES == EXT_STEP // TQ + C_TILES_PER_Q * (seq // TQC - 1)

    bias_a, bias_sel, bias_win, bias_c = _bias_tables(rel_bias, seq)
    et_mat = jnp.asarray(np.arange(seq)[:, None] // SEL_LEN == np.arange(LANES)[None, :], BF16)
    starts = np.arange(nch)[:, None] * CMP_STRIDE
    bstart = np.arange(LANES)[None, :] * SEL_LEN
    ov_mat = jnp.asarray((starts < bstart + SEL_LEN) & (starts + CMP_LEN > bstart), BF16)

    w_in_b = _cast_pack_w_in(w_in)
    w_up_b, w_down_b, w_out_b = _cast_bf16(w_up), _cast_bf16(w_down), _cast_bf16(w_out)
    w_a_b, w_b_b, w_c_b = _cast_bf16(w_branch_a), _cast_bf16(w_branch_b), _cast_bf16(w_branch_c)

    xf = x.reshape(n, d)
    for l in range(depth):
        lam_init = 0.8 - 0.6 * math.exp(-0.3 * l)
        pa, p = _proj(xf, norm_mix_pre[l][None], w_in_b, l)
        p3 = p.reshape(bsz, seq, NCOL)

        a_out = [_attn_a(pa, bias_a[g], g, dil, bsz, seq) for g, (_, dil) in enumerate(DIL_GROUPS)]
        oa = [o for o, _ in a_out]
        lse = [s for _, s in a_out]

        kvc = jnp.stack([p3[:, :, (U_BKV + g * BKV_UNITS + BKV_UNITS - 1) * LANES:(U_BKV + (g + 1) * BKV_UNITS) * LANES]
                         for g in range(B_KV_GROUPS)], axis=1)
        xc = kvc.reshape(bsz, B_KV_GROUPS, seq, 2, HEAD_DIM).transpose(0, 1, 3, 2, 4)
        xc = xc.reshape(bsz * B_KV_GROUPS, 2, nch, CMP_STRIDE * HEAD_DIM)
        half = CMP_STRIDE * HEAD_DIM
        pos = jnp.stack([cmp_k_pos[l], cmp_v_pos[l]]).reshape(2, 2, half)
        w1 = jnp.stack([cmp_k_w1[l], cmp_v_w1[l]]).reshape(2, 2, half, CMP_HIDDEN).astype(BF16)
        w2 = jnp.stack([cmp_k_w2[l], cmp_v_w2[l]]).astype(BF16)
        kcvc = _cmp(xc, pos, w1, w2)
        ob = _nsa(p3, kcvc, bias_sel, bias_win, et_mat, ov_mat).reshape(n, B_HEADS * HEAD_DIM)

        oc = _diff(p3, bias_c, diff_lambda[l], diff_norm[l][None], lam_init).reshape(n, C_HEADS * 2 * HEAD_DIM)

        xf = _mixout(xf, oa, lse, ob, oc, p, w_a_b, w_b_b, w_c_b, w_out_b, norm_mix_post[:, None, :], l)
        xf = _mlp(xf, norm_mlp_pre[l][None], w_up_b, w_down_b, norm_mlp_post[l][None], l)
    return xf.reshape(bsz, seq, d)
```

```python
import functools
import math

import jax
import jax.numpy as jnp
import numpy as np
from jax import lax
from jax.experimental import pallas as pl
from jax.experimental.pallas import tpu as pltpu

F32 = jnp.float32
BF16 = jnp.bfloat16

LANES = 128
HEAD_DIM = 64
SCALE = HEAD_DIM ** -0.5
LOG2E = math.log2(math.e)
LN2 = math.log(2.0)
NORM_EPS = 1e-6
NEG = -1e30
TINY = 1e-30
TQ = 128
TQC = 512
EXT_STEP = 512

DIL_GROUPS = ((128, 1), (512, 4), (2048, 16))
A_HEADS = 8
A_PAIRS = A_HEADS // 2
B_HEADS = 16
B_KV_GROUPS = 2
B_HPG = B_HEADS // B_KV_GROUPS
C_HEADS = 8
CMP_LEN, CMP_STRIDE, CMP_HIDDEN = 32, 16, 256
SEL_LEN, SEL_TOPK, SEL_LOCAL = 64, 16, 2
FORCED_SCORE = 1e6
WIN = 512
N_BUCKETS = 32
BIAS_MAX_DIST = 2048

A_UNITS = 3 * len(DIL_GROUPS) * A_PAIRS
U_MGATE = 0
U_BQ = 48
U_C = 56
U_BKV = 80
BKV_UNITS = 5
U_BGATE = 90
U_TOTAL = 92
NCOL = U_TOTAL * LANES

VMEM_LIMIT = 56 * 1024 * 1024


def _cparams(sem):
    return pltpu.CompilerParams(dimension_semantics=sem, vmem_limit_bytes=VMEM_LIMIT)


def _rms(x, gain):
    return x * lax.rsqrt(jnp.mean(x * x, axis=-1, keepdims=True) + NORM_EPS) * gain


def _dot_nt(a, b):
    return lax.dot_general(a, b, (((1,), (1,)), ((), ())), preferred_element_type=F32)


def _dot(a, b):
    return jnp.dot(a, b, preferred_element_type=F32)


def _lane_half(shape):
    return lax.broadcasted_iota(jnp.int32, shape, len(shape) - 1) < HEAD_DIM


def _stack_halves(tiles):
    low = _lane_half(tiles[0].shape)
    out = []
    for t in tiles:
        z = jnp.zeros_like(t)
        out += [jnp.where(low, t, z), jnp.where(low, z, t)]
    return jnp.concatenate(out, axis=0)


def _softmax_pv(s, v):
    m = jnp.max(s, axis=-1, keepdims=True)
    p = jnp.exp2(s - m)
    l = jnp.sum(p, axis=-1, keepdims=True)
    return _dot(p.astype(BF16), v) / l


def _proj_kernel(x_ref, g_ref, w_ref, oa_ref, o_ref, h_ref, *, n_a):
    j = pl.program_id(1)

    @pl.when(j == 0)
    def _():
        h_ref[...] = _rms(x_ref[...], g_ref[...]).astype(BF16)

    res = _dot(h_ref[...], w_ref[...])

    @pl.when(j < n_a)
    def _():
        for c in range(oa_ref.shape[0]):
            oa_ref[c] = res[:, c * LANES:(c + 1) * LANES]

    @pl.when(j >= n_a)
    def _():
        o_ref[...] = res.astype(o_ref.dtype)


def _proj(x, gain, w, layer, tm=2048, tn=512):
    n, d = x.shape
    n_a = A_UNITS * LANES // tn
    upb = tn // LANES
    return pl.pallas_call(
        functools.partial(_proj_kernel, n_a=n_a),
        out_shape=(jax.ShapeDtypeStruct((A_UNITS, n, LANES), F32), jax.ShapeDtypeStruct((n, NCOL), BF16)),
        grid=(n // tm, w.shape[2] // tn),
        in_specs=[pl.BlockSpec((tm, d), lambda i, j: (i, 0), pipeline_mode=pl.Buffered(1)),
                  pl.BlockSpec((1, d), lambda i, j: (0, 0)),
                  pl.BlockSpec((None, d, tn), lambda i, j: (layer, 0, j))],
        out_specs=(pl.BlockSpec((upb, tm, LANES), lambda i, j: (jnp.minimum(j, n_a - 1), i, 0)),
                   pl.BlockSpec((tm, tn), lambda i, j: (i, jnp.maximum(j - n_a, 0)))),
        scratch_shapes=[pltpu.VMEM((tm, d), BF16)],
        compiler_params=_cparams(("parallel", "arbitrary")),
        name="proj",
    )(x, gain, w)


def _attn_a_kernel(*refs, dil, use_prev):
    if use_prev:
        q_ref, kc_ref, vc_ref, kp_ref, vp_ref, bias_ref, o_ref, lse_ref = refs
    else:
        q_ref, kc_ref, vc_ref, bias_ref, o_ref, lse_ref = refs
    sb = pl.program_id(1)
    r = pl.program_id(2)
    rows = pl.ds(r, TQ, stride=dil) if dil > 1 else slice(None)
    ld = lambda ref, p: ref[p, rows, :].astype(BF16)
    s = []
    for p in range(A_PAIRS):
        qs = _stack_halves([ld(q_ref, p)])
        ks = jnp.concatenate([ld(kp_ref, p), ld(kc_ref, p)], axis=0) if use_prev else ld(kc_ref, p)
        s.append(_dot_nt(qs, ks))
    bias = bias_ref[(sb > 0).astype(jnp.int32)] if use_prev else bias_ref[...]
    s = jnp.concatenate(s, axis=0) + bias
    m = jnp.max(s, axis=-1, keepdims=True)
    e = jnp.exp2(s - m)
    l = jnp.sum(e, axis=-1, keepdims=True)
    pb = e.astype(BF16)
    lse = (m + jnp.log2(l)) * LN2
    low = _lane_half((TQ, LANES))
    for p in range(A_PAIRS):
        r0 = 2 * p * TQ
        vs = jnp.concatenate([ld(vp_ref, p), ld(vc_ref, p)], axis=0) if use_prev else ld(vc_ref, p)
        o = _dot(pb[r0:r0 + 2 * TQ], vs) / l[r0:r0 + 2 * TQ]
        o_ref[p, rows, :] = jnp.where(low, o[:TQ], o[TQ:])
        lse_ref[p, rows, :] = jnp.where(low, lse[r0:r0 + TQ], lse[r0 + TQ:r0 + 2 * TQ])


def _attn_a(pa, bias, g, dil, bsz, seq):
    n = pa.shape[1]
    blk = TQ * dil
    nsb = seq // blk
    use_prev = nsb > 1
    n_grp = len(DIL_GROUPS)
    cur = lambda qkv: pl.BlockSpec((A_PAIRS, blk, LANES), lambda b, sb, r: (qkv * n_grp + g, b * nsb + sb, 0))
    prev = lambda qkv: pl.BlockSpec((A_PAIRS, blk, LANES),
                                    lambda b, sb, r: (qkv * n_grp + g, b * nsb + jnp.maximum(sb - 1, 0), 0))
    out = pl.BlockSpec((A_PAIRS, blk, LANES), lambda b, sb, r: (0, b * nsb + sb, 0))
    in_specs = [cur(0), cur(1), cur(2)] + ([prev(1), prev(2)] if use_prev else [])
    in_specs.append(pl.BlockSpec(bias.shape, lambda b, sb, r: (0,) * bias.ndim))
    args = [pa] * (len(in_specs) - 1) + [bias]
    return pl.pallas_call(
        functools.partial(_attn_a_kernel, dil=dil, use_prev=use_prev),
        out_shape=(jax.ShapeDtypeStruct((A_PAIRS, n, LANES), F32),) * 2,
        grid=(bsz, nsb, dil),
        in_specs=in_specs,
        out_specs=(out, out),
        compiler_params=_cparams(("parallel", "parallel", "arbitrary")),
        name=f"attn_a_d{dil}",
    )(*args)


def _cmp_kernel(x_ref, pos_ref, w1_ref, w2_ref, o_ref):
    for kv in range(2):
        x = x_ref[kv].astype(F32)
        top = (x + pos_ref[kv, 0:1, :]).astype(BF16)
        bot = (x + pos_ref[kv, 1:2, :]).astype(BF16)
        p1 = _dot(top, w1_ref[kv, 0])
        p2 = _dot(bot, w1_ref[kv, 1])
        hid = p1 + pltpu.roll(p2, p2.shape[0] - 1, 0)
        out = _dot(jax.nn.gelu(hid).astype(BF16), w2_ref[kv]).astype(o_ref.dtype)
        o_ref[kv] = jnp.concatenate([out, out], axis=1)


def _cmp(xc, pos, w1, w2):
    nb = xc.shape[0]
    nch, cw = xc.shape[2], xc.shape[3]
    full = lambda a: pl.BlockSpec(a.shape, lambda i: (0,) * a.ndim)
    return pl.pallas_call(
        _cmp_kernel,
        out_shape=jax.ShapeDtypeStruct((nb, 2, nch, LANES), BF16),
        grid=(nb,),
        in_specs=[pl.BlockSpec((None, 2, nch, cw), lambda i: (i, 0, 0, 0)),
                  full(pos), full(w1), full(w2)],
        out_specs=pl.BlockSpec((None, 2, nch, LANES), lambda i: (i, 0, 0, 0)),
        compiler_params=_cparams(("parallel",)),
        name="cmp",
    )(xc, pos, w1, w2)


N_SEL_TILES = 20
N_WIN_TILES = 9
NSA_CHAIN_HEADS = 4


def _nsa_kernel(q_ref, kv_ref, gate_ref, kcvc_ref, bsel_ref, bwin_ref, et_ref, ov_ref, o_ref):
    i = pl.program_id(2)
    seq = kv_ref.shape[0]
    rows = lax.broadcasted_iota(jnp.int32, (TQ, 1), 0)
    t = i * TQ + rows
    q = q_ref[...]
    qs = _stack_halves([q[:, p * LANES:(p + 1) * LANES] for p in range(B_HPG // 2)])
    gates = jax.nn.sigmoid(gate_ref[...].astype(F32))
    head = lambda a, h: a[h * TQ:(h + 1) * TQ]

    kc = kcvc_ref[0]
    vc = kcvc_ref[1]
    n_cmp = kc.shape[0]
    n_idx = lax.broadcasted_iota(jnp.int32, (1, n_cmp), 1)
    cmask = (n_idx * CMP_STRIDE + (CMP_LEN - 1)) <= t
    c_add = jnp.concatenate([jnp.where(cmask, 0.0, NEG)] * B_HPG, axis=0)
    c_mul = jnp.concatenate([cmask.astype(F32)] * B_HPG, axis=0)
    lc = _dot_nt(qs, kc) + c_add
    e = jnp.exp2(lc - jnp.max(lc, axis=-1, keepdims=True)) * c_mul
    p_cmp = e / jnp.maximum(jnp.sum(e, axis=-1, keepdims=True), TINY)
    o_cmp = _dot(p_cmp.astype(BF16), vc)

    def with_selection_mask():
        psum = head(p_cmp, 0)
        for h in range(1, B_HPG):
            psum = psum + head(p_cmp, h)
        hi = psum.astype(BF16)
        lo = (psum - hi.astype(F32)).astype(BF16)
        score = _dot(hi, ov_ref[...]) + _dot(lo, ov_ref[...])
        jb = lax.broadcasted_iota(jnp.int32, (1, LANES), 1)
        back = lax.shift_right_arithmetic(t, int(math.log2(SEL_LEN))) - jb
        forced = (jb == 0) | ((back >= 0) & (back < SEL_LOCAL))
        score = jnp.where(forced, FORCED_SCORE, score)
        score = jnp.where(jb * SEL_LEN <= t, score, NEG)
        rank = jnp.zeros((TQ, LANES), jnp.int32)
        for ii in range(seq // SEL_LEN):
            col = score[:, ii:ii + 1]
            beats = (col > score) | ((col == score) & (ii < jb))
            rank = rank + beats.astype(jnp.int32)
        sel_neg = jnp.where(rank < SEL_TOPK, 0.0, NEG).astype(BF16)
        return jnp.concatenate([qs, jnp.concatenate([sel_neg] * B_HPG, axis=0)], axis=1)

    n_wt = WIN // TQ + 1
    kt0 = jnp.maximum(i - WIN // TQ, 0)
    k0 = pl.multiple_of(kt0 * TQ, TQ)
    w0 = WIN // TQ - i + kt0
    k_w = kv_ref[pl.ds(k0, n_wt * TQ), LANES:2 * LANES]
    v_w = kv_ref[pl.ds(k0, n_wt * TQ), 3 * LANES:4 * LANES]

    def chains(q_all, k, v, bias_ref, first_tile, n_tiles):
        outs = []
        s_all = _dot_nt(q_all, k)
        for h0 in range(0, B_HPG, NSA_CHAIN_HEADS):
            bias = jnp.concatenate(
                [jnp.concatenate([bias_ref[h, first_tile + kk] for kk in range(n_tiles)], axis=1)
                 for h in range(h0, h0 + NSA_CHAIN_HEADS)], axis=0)
            outs.append(_softmax_pv(s_all[h0 * TQ:(h0 + NSA_CHAIN_HEADS) * TQ] + bias, v))
        return jnp.concatenate(outs, axis=0)

    o_win = chains(qs, k_w, v_w, bwin_ref, w0, n_wt)

    def finish(o_sel):
        low = _lane_half((TQ, LANES))
        outs = []
        for h in range(B_HPG):
            outs.append(gates[:, h:h + 1] * head(o_cmp, h)
                        + gates[:, B_HPG + h:B_HPG + h + 1] * head(o_sel, h)
                        + gates[:, 2 * B_HPG + h:2 * B_HPG + h + 1] * head(o_win, h))
        for p in range(B_HPG // 2):
            o_ref[:, p * LANES:(p + 1) * LANES] = jnp.where(low, outs[2 * p], outs[2 * p + 1]).astype(o_ref.dtype)

    tiles_per_ext = EXT_STEP // TQ
    for v in range(seq // EXT_STEP):
        @pl.when(i // tiles_per_ext == v)
        def _(v=v):
            ext = (v + 1) * EXT_STEP
            b0 = (N_SEL_TILES - tiles_per_ext - 1) - i
            if ext <= SEL_TOPK * SEL_LEN:
                q_all, ks = qs, kv_ref[0:ext, 0:LANES]
            else:
                q_all = with_selection_mask()
                ks = jnp.concatenate([kv_ref[0:ext, 0:LANES], et_ref[0:ext, :]], axis=1)
            finish(chains(q_all, ks, kv_ref[0:ext, 2 * LANES:3 * LANES], bsel_ref, b0, ext // TQ))


def _nsa(p3, kcvc, bsel, bwin, et_mat, ov_mat):
    b, s, _ = p3.shape
    nq = s // TQ
    qw = B_HPG * HEAD_DIM
    kvw = BKV_UNITS * LANES
    full = lambda a: pl.BlockSpec(a.shape, lambda g, bb, i: (0,) * a.ndim)
    per_group = lambda a: pl.BlockSpec((B_HPG,) + a.shape[1:], lambda g, bb, i: (g, 0, 0, 0),
                                       pipeline_mode=pl.Buffered(1))
    return pl.pallas_call(
        _nsa_kernel,
        out_shape=jax.ShapeDtypeStruct((b, s, B_HEADS * HEAD_DIM), BF16),
        grid=(B_KV_GROUPS, b, nq),
        in_specs=[
            pl.BlockSpec((None, TQ, qw), lambda g, bb, i: (bb, i, U_BQ * LANES // qw + g)),
            pl.BlockSpec((None, s, kvw), lambda g, bb, i: (bb, 0, U_BKV * LANES // kvw + g)),
            pl.BlockSpec((None, TQ, LANES), lambda g, bb, i: (bb, i, U_BGATE + g)),
            pl.BlockSpec((None, 2, kcvc.shape[2], LANES), lambda g, bb, i: (bb * B_KV_GROUPS + g, 0, 0, 0)),
            per_group(bsel), per_group(bwin), full(et_mat), full(ov_mat)],
        out_specs=pl.BlockSpec((None, TQ, qw), lambda g, bb, i: (bb, i, g)),
        compiler_params=_cparams(("parallel", "parallel", "arbitrary")),
        name="nsa",
    )(p3, p3, p3, kcvc, bsel, bwin, et_mat, ov_mat)


C_TILES_PER_Q = TQC // TQ
DIFF_CHAIN_ROWS = 256
N_C_TILES = 16


def _diff_kernel(q_ref, k_ref, v_ref, bias_ref, lam_ref, gain_ref, o_ref, *, lam_init):
    i = pl.program_id(2)
    seq = k_ref.shape[0]
    lv = lam_ref[...]
    lam = (jnp.exp(jnp.sum(lv[0:1] * lv[1:2], axis=-1, keepdims=True))
           - jnp.exp(jnp.sum(lv[2:3] * lv[3:4], axis=-1, keepdims=True)) + lam_init)
    qs = _stack_halves([q_ref[...]])
    q_per_ext = EXT_STEP // TQC
    for v in range(seq // EXT_STEP):
        @pl.when(i // q_per_ext == v)
        def _(v=v):
            ext = (v + 1) * EXT_STEP
            b0 = (N_C_TILES - EXT_STEP // TQ) - C_TILES_PER_Q * i
            bias = jnp.concatenate([bias_ref[b0 + kk] for kk in range(ext // TQ)], axis=1)
            bias2 = jnp.concatenate([bias, bias], axis=0)
            o = jnp.concatenate(
                [_softmax_pv(_dot_nt(qs[r0:r0 + DIFF_CHAIN_ROWS], k_ref[0:ext, :]) + bias2[r0:r0 + DIFF_CHAIN_ROWS],
                             v_ref[0:ext, :]) for r0 in range(0, 2 * TQC, DIFF_CHAIN_ROWS)], axis=0)
            o = o[:TQC] - lam * o[TQC:]
            o_ref[...] = (_rms(o, gain_ref[...]) * (1.0 - lam_init)).astype(o_ref.dtype)


def _diff(p3, bias, lam_vecs, sub_gain, lam_init):
    b, s, _ = p3.shape
    dv = 2 * HEAD_DIM
    full = lambda a: pl.BlockSpec(a.shape, lambda h, bb, i: (0,) * a.ndim)
    return pl.pallas_call(
        functools.partial(_diff_kernel, lam_init=lam_init),
        out_shape=jax.ShapeDtypeStruct((b, s, C_HEADS * dv), BF16),
        grid=(C_HEADS, b, s // TQC),
        in_specs=[
            pl.BlockSpec((None, TQC, dv), lambda h, bb, i: (bb, i, U_C + h)),
            pl.BlockSpec((None, s, dv), lambda h, bb, i: (bb, 0, U_C + C_HEADS + h)),
            pl.BlockSpec((None, s, dv), lambda h, bb, i: (bb, 0, U_C + 2 * C_HEADS + h)),
            pl.BlockSpec((None,) + bias.shape[1:], lambda h, bb, i: (h, 0, 0, 0)),
            full(lam_vecs), full(sub_gain)],
        out_specs=pl.BlockSpec((None, TQC, dv), lambda h, bb, i: (bb, i, h)),
        compiler_params=_cparams(("parallel", "parallel", "arbitrary")),
        name="diff",
    )(p3, p3, p3, bias, lam_vecs, sub_gain)


def _mixout_kernel(x_ref, oa0_ref, oa1_ref, oa2_ref, l0_ref, l1_ref, l2_ref, ob_ref, oc_ref,
                   g0_ref, g1_ref, g2_ref, wa_ref, wb_ref, wc_ref, wo_ref, gain_ref, o_ref):
    planes = lambda ref: jnp.concatenate([ref[p] for p in range(ref.shape[0])], axis=1)
    lse = [planes(l0_ref), planes(l1_ref), planes(l2_ref)]
    m = jnp.maximum(jnp.maximum(lse[0], lse[1]), lse[2])
    e = [jnp.exp(v - m) for v in lse]
    den = e[0] + e[1] + e[2]
    oa = (e[0] / den) * planes(oa0_ref) + (e[1] / den) * planes(oa1_ref) + (e[2] / den) * planes(oa2_ref)
    mixed = (jax.nn.sigmoid(g0_ref[...].astype(F32)) * _dot(oa.astype(BF16), wa_ref[...])
             + jax.nn.sigmoid(g1_ref[...].astype(F32)) * _dot(ob_ref[...], wb_ref[...])
             + jax.nn.sigmoid(g2_ref[...].astype(F32)) * _dot(oc_ref[...], wc_ref[...]))
    y = _dot(mixed.astype(BF16), wo_ref[...])
    o_ref[...] = x_ref[...] + _rms(y, gain_ref[...])


def _mixout(x, oa, lse, ob, oc, p, wa, wb, wc, wo, gain, layer, tm=256):
    n, d = x.shape
    const = lambda a: pl.BlockSpec((None,) + a.shape[1:], lambda i: (layer,) + (0,) * (a.ndim - 1),
                                   pipeline_mode=pl.Buffered(1))
    row = lambda a: pl.BlockSpec((tm, a.shape[1]), lambda i: (i, 0))
    grp = lambda a: pl.BlockSpec((a.shape[0], tm, a.shape[2]), lambda i: (0, i, 0))
    gate = lambda br: pl.BlockSpec((tm, d), lambda i: (i, U_MGATE * LANES // d + br))
    return pl.pallas_call(
        _mixout_kernel,
        out_shape=jax.ShapeDtypeStruct((n, d), F32),
        grid=(n // tm,),
        in_specs=[row(x)] + [grp(a) for a in oa] + [grp(a) for a in lse] + [row(ob), row(oc),
                  gate(0), gate(1), gate(2), const(wa), const(wb), const(wc), const(wo), const(gain)],
        out_specs=row(x),
        compiler_params=_cparams(("parallel",)),
        name="mixout",
    )(x, *oa, *lse, ob, oc, p, p, p, wa, wb, wc, wo, gain)


def _mlp_kernel(x_ref, gpre_ref, wup_ref, wdn_ref, gpost_ref, o_ref, h_ref, acc_ref):
    f = pl.program_id(1)

    @pl.when(f == 0)
    def _():
        h_ref[...] = _rms(x_ref[...], gpre_ref[...]).astype(BF16)
        acc_ref[...] = jnp.zeros_like(acc_ref)

    u = jnp.square(jax.nn.relu(_dot(h_ref[...], wup_ref[...])))
    acc_ref[...] += _dot(u.astype(BF16), wdn_ref[...])

    @pl.when(f == pl.num_programs(1) - 1)
    def _():
        o_ref[...] = x_ref[...] + _rms(acc_ref[...], gpost_ref[...])


def _mlp(x, gpre, wup, wdn, gpost, layer, tm=512, tf=1024):
    n, d = x.shape
    dff = wup.shape[2]
    return pl.pallas_call(
        _mlp_kernel,
        out_shape=jax.ShapeDtypeStruct((n, d), F32),
        grid=(n // tm, dff // tf),
        in_specs=[pl.BlockSpec((tm, d), lambda i, f: (i, 0)),
                  pl.BlockSpec((1, d), lambda i, f: (0, 0)),
                  pl.BlockSpec((None, d, tf), lambda i, f: (layer, 0, f)),
                  pl.BlockSpec((None, tf, d), lambda i, f: (layer, f, 0)),
                  pl.BlockSpec((1, d), lambda i, f: (0, 0))],
        out_specs=pl.BlockSpec((tm, d), lambda i, f: (i, 0)),
        scratch_shapes=[pltpu.VMEM((tm, d), BF16), pltpu.VMEM((tm, d), F32)],
        compiler_params=_cparams(("parallel", "arbitrary")),
        name="mlp",
    )(x, gpre, wup, wdn, gpost)


def _cast_kernel(w_ref, o_ref):
    o_ref[...] = w_ref[...].astype(o_ref.dtype)


def _cast_bf16(w, block_bytes=8 * 1024 * 1024):
    nl, rows, cols = w.shape
    tr = max(8, min(rows, block_bytes // (4 * cols) // 8 * 8))
    while rows % tr:
        tr -= 8
    return pl.pallas_call(
        _cast_kernel,
        out_shape=jax.ShapeDtypeStruct(w.shape, BF16),
        grid=(nl, rows // tr),
        in_specs=[pl.BlockSpec((None, tr, cols), lambda l, i: (l, i, 0))],
        out_specs=pl.BlockSpec((None, tr, cols), lambda l, i: (l, i, 0)),
        compiler_params=_cparams(("parallel", "parallel")),
        name="cast",
    )(w)


def _toeplitz_kernel(v_ref, o_ref):
    n_tiles, rows, _ = o_ref.shape
    x = jnp.broadcast_to(v_ref[...], (rows, v_ref.shape[-1]))
    t = pltpu.roll(x, 0, 1, stride=1, stride_axis=0)
    for k in range(n_tiles):
        o_ref[k] = t[:, k * TQ:(k + 1) * TQ]


def _toeplitz_tiles(per_dist, dist_of_m, rows, n_tiles):
    width = n_tiles * TQ
    w = width + rows
    m = np.arange(w)
    m = np.where(m >= width, m - w, m)
    dist = dist_of_m(m)
    neg_col = per_dist.shape[1]
    idx = np.where(dist >= 0, np.minimum(dist, neg_col - 1), neg_col)
    heads = per_dist.shape[0]
    vec = jnp.concatenate([per_dist, jnp.full((heads, 1), NEG, F32)], axis=1)[:, idx]
    return pl.pallas_call(
        _toeplitz_kernel,
        out_shape=jax.ShapeDtypeStruct((heads, n_tiles, rows, TQ), F32),
        grid=(heads,),
        in_specs=[pl.BlockSpec((None, 1, w), lambda h: (h, 0, 0))],
        out_specs=pl.BlockSpec((None, n_tiles, rows, TQ), lambda h: (h, 0, 0, 0)),
        compiler_params=_cparams(("parallel",)),
        name="toeplitz",
    )(vec.reshape(heads, 1, w))


def _t5_bucket(dist):
    exact = N_BUCKETS // 2
    d = jnp.maximum(dist, 0)
    logd = jnp.log(jnp.maximum(d, 1).astype(F32) / exact)
    far = exact + (logd / math.log(BIAS_MAX_DIST / exact) * (N_BUCKETS - exact)).astype(jnp.int32)
    return jnp.where(d < exact, d, jnp.minimum(far, N_BUCKETS - 1))


def _bias_tables(rel_bias, seq):
    per_dist = rel_bias[_t5_bucket(jnp.arange(seq + 1))].T * LOG2E
    n_a = len(DIL_GROUPS) * A_HEADS
    bias_a = []
    for g, (_, dil) in enumerate(DIL_GROUPS):
        t = _toeplitz_tiles(per_dist[g * A_HEADS:(g + 1) * A_HEADS],
                            lambda m, dil=dil: np.where((m >= 0) & (m <= TQ), dil * (TQ - m), -1), TQ, 2)
        if seq // (TQ * dil) > 1:
            with_prev = t.transpose(0, 2, 1, 3).reshape(A_HEADS * TQ, 2 * TQ)
            no_prev = jnp.where(np.arange(2 * TQ)[None, :] < TQ, NEG, with_prev)
            bias_a.append(jnp.stack([no_prev, with_prev]))
        else:
            bias_a.append(t[:, 1].reshape(A_HEADS * TQ, TQ))
    pb = per_dist[n_a:n_a + B_HEADS]
    off_s = (N_SEL_TILES - EXT_STEP // TQ - 1) * TQ
    bias_sel = _toeplitz_tiles(pb, lambda m: off_s - m, TQ, N_SEL_TILES)
    bias_win = _toeplitz_tiles(pb, lambda m: np.where(WIN - m < WIN, WIN - m, -1), TQ, N_WIN_TILES)
    off_c = (N_C_TILES - EXT_STEP // TQ) * TQ
    bias_c = _toeplitz_tiles(per_dist[n_a + B_HEADS:], lambda m: off_c - m, TQC, N_C_TILES)
    return bias_a, bias_sel, bias_win, bias_c


def _w_in_segments(d):
    widths = (A_UNITS * LANES, B_HEADS * HEAD_DIM, 6 * B_KV_GROUPS * HEAD_DIM, 3 * B_HEADS,
              3 * C_HEADS * 2 * HEAD_DIM, 3 * d)
    return tuple(int(v) for v in np.cumsum((0,) + widths))


def _cast_pack_kernel(w_ref, pkv_ref, pg_ref, o_ref, *, seg):
    _, o_bq, o_bkv, o_bg, o_c, o_mg, end = seg
    q_scale = SCALE * LOG2E

    def t(lo, hi, n_q=0):
        xt = w_ref[lo:hi, :].T
        if n_q == hi - lo:
            xt = xt * q_scale
        elif n_q:
            xt = jnp.concatenate([xt[:, :n_q] * q_scale, xt[:, n_q:]], axis=1)
        return xt.astype(BF16)

    bkv = _dot(t(o_bkv, o_bg), pkv_ref[...]).astype(BF16)
    bg = _dot(t(o_bg, o_bg + LANES), pg_ref[...]).astype(BF16)
    o_ref[...] = jnp.concatenate([t(0, o_bq, o_bq // 3), t(o_mg, end), t(o_bq, o_bkv, o_bkv - o_bq),
                                  t(o_c, o_mg, (o_mg - o_c) // 3), bkv, bg], axis=1)


def _cast_pack_w_in(w_in):
    nl, d, cols = w_in.shape
    seg = _w_in_segments(d)
    _, o_bq, o_bkv, o_bg, o_c, _, _ = seg
    order = (2, 2, 4, 4, 3, 3, 5, 5, 0, 1)
    pkv = np.zeros((o_bg - o_bkv, B_KV_GROUPS * BKV_UNITS * LANES), np.float32)
    for g in range(B_KV_GROUPS):
        for slot, src in enumerate(order):
            for c in range(HEAD_DIM):
                pkv[(src * B_KV_GROUPS + g) * HEAD_DIM + c, (g * len(order) + slot) * HEAD_DIM + c] = 1.0
    pg = np.zeros((LANES, B_KV_GROUPS * LANES), np.float32)
    for g in range(B_KV_GROUPS):
        for h in range(B_HPG):
            for br in range(3):
                pg[(g * B_HPG + h) * 3 + br, g * LANES + br * B_HPG + h] = 1.0
    full = lambda a: pl.BlockSpec(a.shape, lambda l, i: (0,) * a.ndim)
    pkv, pg = jnp.asarray(pkv, BF16), jnp.asarray(pg, BF16)
    return pl.pallas_call(
        functools.partial(_cast_pack_kernel, seg=seg),
        out_shape=jax.ShapeDtypeStruct((nl, d, o_bq + NCOL), BF16),
        grid=(nl, d // LANES),
        in_specs=[pl.BlockSpec((None, cols, LANES), lambda l, i: (l, 0, i)), full(pkv), full(pg)],
        out_specs=pl.BlockSpec((None, LANES, o_bq + NCOL), lambda l, i: (l, i, 0)),
        compiler_params=_cparams(("parallel", "parallel")),
        name="cast_pack",
    )(jnp.swapaxes(w_in, 1, 2), pkv, pg)


def kernel(x, rel_bias, w_in, cmp_k_pos, cmp_v_pos, cmp_k_w1, cmp_k_w2, cmp_v_w1, cmp_v_w2, diff_lambda, diff_norm, w_branch_a, w_branch_b, w_branch_c, w_out, norm_mix_pre, norm_mix_post, norm_mlp_pre, norm_mlp_post, w_up, w_down):
    bsz, seq, d = x.shape
    n = bsz * seq
    depth = w_in.shape[0]
    nch = seq // CMP_STRIDE
    assert seq % EXT_STEP == 0 and seq // TQ == N_SEL_TILES - EXT_STEP // TQ and (U_MGATE * LANES) % d == 0
    assert N_C_TILES == EXT_STEP // TQ + C_TILES_PER_Q * (seq // TQC - 1)

    bias_a, bias_sel, bias_win, bias_c = _bias_tables(rel_bias, seq)
    et_mat = jnp.asarray(np.arange(seq)[:, None] // SEL_LEN == np.arange(LANES)[None, :], BF16)
    starts = np.arange(nch)[:, None] * CMP_STRIDE
    bstart = np.arange(LANES)[None, :] * SEL_LEN
    ov_mat = jnp.asarray((starts < bstart + SEL_LEN) & (starts + CMP_LEN > bstart), BF16)

    w_in_b = _cast_pack_w_in(w_in)
    w_up_b, w_down_b, w_out_b = _cast_bf16(w_up), _cast_bf16(w_down), _cast_bf16(w_out)
    w_a_b, w_b_b, w_c_b = _cast_bf16(w_branch_a), _cast_bf16(w_branch_b), _cast_bf16(w_branch_c)

    xf = x.reshape(n, d)
    for l in range(depth):
        lam_init = 0.8 - 0.6 * math.exp(-0.3 * l)
        pa, p = _proj(xf, norm_mix_pre[l][None], w_in_b, l)
        p3 = p.reshape(bsz, seq, NCOL)

        a_out = [_attn_a(pa, bias_a[g], g, dil, bsz, seq) for g, (_, dil) in enumerate(DIL_GROUPS)]
        oa = [o for o, _ in a_out]
        lse = [s for _, s in a_out]

        kvc = jnp.stack([p3[:, :, (U_BKV + g * BKV_UNITS + BKV_UNITS - 1) * LANES:(U_BKV + (g + 1) * BKV_UNITS) * LANES]
                         for g in range(B_KV_GROUPS)], axis=1)
        xc = kvc.reshape(bsz, B_KV_GROUPS, seq, 2, HEAD_DIM).transpose(0, 1, 3, 2, 4)
        xc = xc.reshape(bsz * B_KV_GROUPS, 2, nch, CMP_STRIDE * HEAD_DIM)
        half = CMP_STRIDE * HEAD_DIM
        pos = jnp.stack([cmp_k_pos[l], cmp_v_pos[l]]).reshape(2, 2, half)
        w1 = jnp.stack([cmp_k_w1[l], cmp_v_w1[l]]).reshape(2, 2, half, CMP_HIDDEN).astype(BF16)
        w2 = jnp.stack([cmp_k_w2[l], cmp_v_w2[l]]).astype(BF16)
        kcvc = _cmp(xc, pos, w1, w2)
        ob = _nsa(p3, kcvc, bias_sel, bias_win, et_mat, ov_mat).reshape(n, B_HEADS * HEAD_DIM)

        oc = _diff(p3, bias_c, diff_lambda[l], diff_norm[l][None], lam_init).reshape(n, C_HEADS * 2 * HEAD_DIM)

        xf = _mixout(xf, oa, lse, ob, oc, p, w_a_b, w_b_b, w_c_b, w_out_b, norm_mix_post[:, None, :], l)
        xf = _mlp(xf, norm_mlp_pre[l][None], w_up_b, w_down_b, norm_mlp_post[l][None], l)
    return xf.reshape(bsz, seq, d)
```
